```python
import jax, jax.numpy as jnp
from jax import lax
import numpy as np

D_MODEL = 1024
BATCH = 8
SEQ = 2048
DEPTH = 4

GRID_W = 64
CTX_LEN = 256
HEAD_DIM = 64
NA_HEADS = 4
GA_HEADS = 8
GA_KV_HEADS = 2
WA_HEADS = 4
WA_KV_HEADS = 2
D_MIX = (NA_HEADS + GA_HEADS + WA_HEADS) * HEAD_DIM
NA_WIN_ROWS = 8
NA_WIN_COLS = 16
WA_RADIUS = 128
Q_BLOCK = 128
D_FF = 2816
CONV_W = 3
ROPE_THETA = 10000.0
ROPE_PAIRS_PER_AXIS = HEAD_DIM // 4
EPS = 1e-6
N_MOD = 6

PROJ_SIZES = (NA_HEADS * HEAD_DIM, NA_HEADS * HEAD_DIM, NA_HEADS * HEAD_DIM,
              GA_HEADS * HEAD_DIM, GA_KV_HEADS * HEAD_DIM, GA_KV_HEADS * HEAD_DIM,
              WA_HEADS * HEAD_DIM, WA_KV_HEADS * HEAD_DIM, WA_KV_HEADS * HEAD_DIM)
PROJ_SPLITS = tuple(int(s) for s in np.cumsum(PROJ_SIZES)[:-1])
D_PROJ = int(sum(PROJ_SIZES))

kernel_name = "hymba_style_parallel_heads_dit_block"


def rmsnorm(x, g):
    xf = x.astype(jnp.float32)
    y = xf * lax.rsqrt(jnp.mean(xf * xf, axis=-1, keepdims=True) + EPS)
    return (y * g.astype(jnp.float32)).astype(x.dtype)


def modulate(h, shift, scale):
    return h * (1 + scale) + shift


def axial_rope(n_tokens):
    t = jnp.arange(n_tokens, dtype=jnp.int32)
    row = (t // GRID_W).astype(jnp.float32)
    col = (t % GRID_W).astype(jnp.float32)
    inv = ROPE_THETA ** (-jnp.arange(ROPE_PAIRS_PER_AXIS, dtype=jnp.float32) / ROPE_PAIRS_PER_AXIS)
    ang = jnp.concatenate([row[:, None] * inv, col[:, None] * inv], axis=-1)
    return jnp.cos(ang), jnp.sin(ang)


def apply_rope(x, cos, sin):
    xf = x.astype(jnp.float32)
    half = HEAD_DIM // 2
    x1, x2 = xf[..., :half], xf[..., half:]
    c = cos[None, :, None, :]
    s = sin[None, :, None, :]
    return jnp.concatenate([x1 * c - x2 * s, x1 * s + x2 * c], axis=-1).astype(x.dtype)


def dense_attn(q, k, v, sink=None):
    B, Lq, H, d = q.shape
    KVH = k.shape[2]
    G = H // KVH
    qg = q.reshape(B, Lq, KVH, G, d)
    s = jnp.einsum('bqkgd,bskd->bkgqs', qg, k).astype(jnp.float32) * (d ** -0.5)
    if sink is not None:
        sk = jnp.broadcast_to(sink.reshape(KVH, G)[None, :, :, None, None].astype(jnp.float32),
                              s.shape[:-1] + (1,))
        p = jax.nn.softmax(jnp.concatenate([s, sk], axis=-1), axis=-1)[..., :-1]
    else:
        p = jax.nn.softmax(s, axis=-1)
    o = jnp.einsum('bkgqs,bskd->bqkgd', p.astype(v.dtype), v)
    return o.reshape(B, Lq, H, d)


def neighbourhood_attn(q, k, v, kc, vc, rpb):
    B, S, H, d = q.shape
    rows = S // GRID_W
    kr = min(NA_WIN_ROWS, rows)
    kw = NA_WIN_COLS
    qg = q.reshape(B, rows, GRID_W, H, d)
    kg = k.reshape(B, rows, GRID_W, H, d)
    vg = v.reshape(B, rows, GRID_W, H, d)
    col = jnp.arange(GRID_W)
    col_idx = jnp.clip(col - kw // 2, 0, GRID_W - kw)[:, None] + jnp.arange(kw)[None]
    dc = col_idx - col[:, None]
    row = jnp.arange(rows)
    row_start = jnp.clip(row - kr // 2, 0, rows - kr)
    scale = d ** -0.5

    def one_row(args):
        r, rs, q_r = args
        k_nb = lax.dynamic_slice_in_dim(kg, rs, kr, axis=1)[:, :, col_idx]
        v_nb = lax.dynamic_slice_in_dim(vg, rs, kr, axis=1)[:, :, col_idx]
        dr = rs + jnp.arange(kr) - r
        bias = rpb[:, dr[:, None, None] + NA_WIN_ROWS - 1, dc[None] + NA_WIN_COLS - 1]
        bias = bias.transpose(0, 2, 1, 3).astype(jnp.float32)
        s_loc = jnp.einsum('bwhd,brwchd->bhwrc', q_r, k_nb).astype(jnp.float32) * scale + bias[None]
        s_loc = s_loc.reshape(B, H, GRID_W, kr * kw)
        s_ctx = jnp.einsum('bwhd,bchd->bhwc', q_r, kc).astype(jnp.float32) * scale
        p = jax.nn.softmax(jnp.concatenate([s_loc, s_ctx], axis=-1), axis=-1).astype(v.dtype)
        p_loc = p[..., :kr * kw].reshape(B, H, GRID_W, kr, kw)
        p_ctx = p[..., kr * kw:]
        return (jnp.einsum('bhwrc,brwchd->bwhd', p_loc, v_nb)
                + jnp.einsum('bhwc,bchd->bwhd', p_ctx, vc))

    o = lax.map(one_row, (row, row_start, qg.swapaxes(0, 1)))
    return o.swapaxes(0, 1).reshape(B, S, H, d)


def global_gqa(q, k, v, kc, vc):
    B, S, H, d = q.shape
    k_all = jnp.concatenate([k, kc], axis=1)
    v_all = jnp.concatenate([v, vc], axis=1)
    nb = S // Q_BLOCK
    qb = q.reshape(B, nb, Q_BLOCK, H, d).swapaxes(0, 1)
    o = lax.map(lambda qi: dense_attn(qi, k_all, v_all), qb)
    return o.swapaxes(0, 1).reshape(B, S, H, d)


def window_gqa(q, k, v, kc, vc, sink):
    B, S, H, d = q.shape
    KVH = k.shape[2]
    G = H // KVH
    nb = S // Q_BLOCK
    span = Q_BLOCK + 2 * WA_RADIUS
    pad = ((0, 0), (WA_RADIUS, WA_RADIUS), (0, 0), (0, 0))
    kp = jnp.pad(k, pad)
    vp = jnp.pad(v, pad)
    idx = (jnp.arange(nb) * Q_BLOCK)[:, None] + jnp.arange(span)[None]
    kb = kp[:, idx]
    vb = vp[:, idx]
    kpos = idx - WA_RADIUS
    qpos = jnp.arange(S).reshape(nb, Q_BLOCK)
    valid = ((jnp.abs(qpos[:, :, None] - kpos[:, None, :]) <= WA_RADIUS)
             & (kpos[:, None, :] >= 0) & (kpos[:, None, :] < S))
    qb = q.reshape(B, nb, Q_BLOCK, KVH, G, d)
    scale = d ** -0.5
    s_loc = jnp.einsum('bnqkgd,bnskd->bnkgqs', qb, kb).astype(jnp.float32) * scale
    s_loc = jnp.where(valid[None, :, None, None], s_loc, -jnp.inf)
    s_ctx = jnp.einsum('bnqkgd,bckd->bnkgqc', qb, kc).astype(jnp.float32) * scale
    s_sink = jnp.broadcast_to(sink.reshape(KVH, G)[None, None, :, :, None, None].astype(jnp.float32),
                              s_loc.shape[:-1] + (1,))
    p = jax.nn.softmax(jnp.concatenate([s_loc, s_ctx, s_sink], axis=-1), axis=-1).astype(v.dtype)
    L = kc.shape[1]
    o = (jnp.einsum('bnkgqs,bnskd->bnqkgd', p[..., :span], vb)
         + jnp.einsum('bnkgqc,bckd->bnqkgd', p[..., span:span + L], vc))
    return o.reshape(B, S, H, d)


def token_mix(h, hc, w_in, q_gain, k_gain, rpb, sink, w_out, cos, sin, with_ctx_out):
    B, S, _ = h.shape
    L = hc.shape[1]
    heads = lambda t: t.reshape(t.shape[:-1] + (-1, HEAD_DIM))
    qa, ka, va, qb, kb, vb, qw, kw, vw = [heads(t) for t in jnp.split(h @ w_in, PROJ_SPLITS, axis=-1)]
    qca, kca, vca, qcb, kcb, vcb, qcw, kcw, vcw = [heads(t) for t in jnp.split(hc @ w_in, PROJ_SPLITS, axis=-1)]
    qb = apply_rope(rmsnorm(qb, q_gain), cos, sin)
    kb = apply_rope(rmsnorm(kb, k_gain), cos, sin)
    kcb = rmsnorm(kcb, k_gain)
    qw = apply_rope(qw, cos, sin)
    kw = apply_rope(kw, cos, sin)
    oa = neighbourhood_attn(qa, ka, va, kca, vca, rpb)
    ob = global_gqa(qb, kb, vb, kcb, vcb)
    ow = window_gqa(qw, kw, vw, kcw, vcw, sink)
    out = jnp.concatenate([oa, ob, ow], axis=2).reshape(B, S, D_MIX) @ w_out
    if not with_ctx_out:
        return out, None
    oca = dense_attn(qca, kca, vca)
    ocb = dense_attn(rmsnorm(qcb, q_gain), kcb, vcb)
    ocw = dense_attn(qcw, kcw, vcw, sink)
    out_c = jnp.concatenate([oca, ocb, ocw], axis=2).reshape(B, L, D_MIX) @ w_out
    return out, out_c


def conv_ffn(h, w_up, conv_w, conv_b, w_down):
    a, b = jnp.split(h @ w_up, 2, axis=-1)
    n = a.shape[1]
    half = CONV_W // 2
    ap = jnp.pad(a, ((0, 0), (half, half), (0, 0)))
    a = sum(ap[:, j:j + n] * conv_w[j] for j in range(CONV_W)) + conv_b
    return (jax.nn.silu(a) * b) @ w_down


def setup_inputs(seed: int = 0) -> dict:
    key = jax.random.key(seed)
    ks = jax.random.split(key, 20)
    f32 = jnp.float32
    nrm = lambda k, shape, s: jax.random.normal(k, shape, f32) * s
    D = D_MODEL
    return {
        "x": nrm(ks[0], (BATCH, SEQ, D), 1.0),
        "c": nrm(ks[1], (BATCH, D), 1.0),
        "ctx": nrm(ks[2], (BATCH, CTX_LEN, D), 1.0),
        "c_ctx": nrm(ks[3], (D,), 1.0),
        "attn_norm": 1.0 + nrm(ks[4], (DEPTH, D), 0.05),
        "ffn_norm": 1.0 + nrm(ks[5], (DEPTH, D), 0.05),
        "w_mod": nrm(ks[6], (DEPTH, D, N_MOD * D), 0.5 * D ** -0.5),
        "b_mod": nrm(ks[7], (DEPTH, N_MOD * D), 0.02),
        "w_in": nrm(ks[8], (DEPTH, D, D_PROJ), D ** -0.5),
        "q_gain": 1.0 + nrm(ks[9], (DEPTH, HEAD_DIM), 0.05),
        "k_gain": 1.0 + nrm(ks[10], (DEPTH, HEAD_DIM), 0.05),
        "na_rpb": nrm(ks[11], (DEPTH, NA_HEADS, 2 * NA_WIN_ROWS - 1, 2 * NA_WIN_COLS - 1), 0.1),
        "wa_sink": nrm(ks[12], (DEPTH, WA_HEADS), 0.5),
        "w_out": nrm(ks[13], (DEPTH, D_MIX, D), D_MIX ** -0.5),
        "w_up": nrm(ks[14], (DEPTH, D, 2 * D_FF), D ** -0.5),
        "conv_w": nrm(ks[15], (DEPTH, CONV_W, D_FF), CONV_W ** -0.5),
        "conv_b": nrm(ks[16], (DEPTH, D_FF), 0.01),
        "w_down": nrm(ks[17], (DEPTH, D_FF, D), D_FF ** -0.5),
        "final_norm": 1.0 + nrm(ks[18], (D,), 0.05),
    }


def reference(x, c, ctx, c_ctx, attn_norm, ffn_norm, w_mod, b_mod, w_in, q_gain, k_gain,
              na_rpb, wa_sink, w_out, w_up, conv_w, conv_b, w_down, final_norm):
    S = x.shape[1]
    cos, sin = axial_rope(S)
    c_act = jax.nn.silu(c)
    cc_act = jax.nn.silu(c_ctx)
    cx = ctx
    for l in range(DEPTH):
        last = l == DEPTH - 1
        mx = (c_act @ w_mod[l] + b_mod[l])[:, None, :]
        mc = cc_act @ w_mod[l] + b_mod[l]
        sh_a, sc_a, gt_a, sh_f, sc_f, gt_f = jnp.split(mx, N_MOD, axis=-1)
        csh_a, csc_a, cgt_a, csh_f, csc_f, cgt_f = jnp.split(mc, N_MOD, axis=-1)
        h = modulate(rmsnorm(x, attn_norm[l]), sh_a, sc_a)
        hc = modulate(rmsnorm(cx, attn_norm[l]), csh_a, csc_a)
        o, oc = token_mix(h, hc, w_in[l], q_gain[l], k_gain[l], na_rpb[l], wa_sink[l], w_out[l],
                          cos, sin, not last)
        x = x + gt_a * o
        x = x + gt_f * conv_ffn(modulate(rmsnorm(x, ffn_norm[l]), sh_f, sc_f),
                                w_up[l], conv_w[l], conv_b[l], w_down[l])
        if not last:
            cx = cx + cgt_a * oc
            cx = cx + cgt_f * conv_ffn(modulate(rmsnorm(cx, ffn_norm[l]), csh_f, csc_f),
                                       w_up[l], conv_w[l], conv_b[l], w_down[l])
    return rmsnorm(x, final_norm)
```

```python
import functools

import numpy as np
import jax
import jax.numpy as jnp
from jax import lax
from jax.experimental import pallas as pl
from jax.experimental.pallas import tpu as pltpu

D_MODEL = 1024
BATCH = 8
SEQ = 2048
DEPTH = 4
GRID_W = 64
CTX_LEN = 256
HEAD_DIM = 64
NA_HEADS = 4
GA_HEADS = 8
GA_KV_HEADS = 2
WA_HEADS = 4
WA_KV_HEADS = 2
NA_WIN_ROWS = 8
NA_WIN_COLS = 16
WA_RADIUS = 128
D_FF = 2816
CONV_W = 3
ROPE_THETA = 10000.0
EPS = 1e-6
N_MOD = 6
D_PROJ = 2048

ROWS = SEQ + CTX_LEN
TM = 256
N_BLK = ROWS // TM
N_LAT = SEQ // TM
GRID_H = SEQ // GRID_W
LANES = 128
HALO = 8
MOD_ROWS = 16
NEG = -1e30
VMEM_LIMIT = 56 * 1024 * 1024

F32 = jnp.float32
BF16 = jnp.bfloat16


def _params(n_axes, vmem=None):
    return pltpu.CompilerParams(dimension_semantics=("arbitrary",) * n_axes,
                                vmem_limit_bytes=vmem)


def _rope_lane_perm(base, n_heads_per_chunk_stride, n_chunks):
    cols = []
    for c in range(n_chunks):
        for lane in range(LANES):
            part, hsel, i = lane // 64, (lane // 32) % 2, lane % 32
            head = c + n_heads_per_chunk_stride * hsel
            cols.append(base + HEAD_DIM * head + 32 * part + i)
    return cols


def _in_proj_perm():
    qa, ka, va = 0, 256, 512
    qb, kb, vb = 768, 1280, 1408
    qw, kw, vw = 1536, 1792, 1920
    cols = []
    cols += _rope_lane_perm(qb, 4, 4)
    cols += list(range(qa, qa + 256))
    cols += list(range(ka, ka + 256))
    cols += list(range(va, va + 256))
    cols += _rope_lane_perm(qw, 2, 2)
    cols += _rope_lane_perm(kb, 1, 1)
    cols += list(range(vb, vb + 128))
    cols += _rope_lane_perm(kw, 1, 1)
    cols += list(range(vw, vw + 128))
    return np.asarray(cols, np.int32)


def _out_proj_perm():
    rows = list(range(256))
    for c in range(4):
        for lane in range(LANES):
            rows.append(256 + HEAD_DIM * (c + 4 * (lane // 64)) + lane % 64)
    for c in range(2):
        for lane in range(LANES):
            rows.append(768 + HEAD_DIM * (c + 2 * (lane // 64)) + lane % 64)
    return np.asarray(rows, np.int32)


_IN_PERM = _in_proj_perm()
_OUT_PERM = _out_proj_perm()
_GAIN_PERM = np.asarray([32 * (l // 64) + l % 32 for l in range(LANES)], np.int32)


def _rope_tables():
    t = jnp.arange(SEQ, dtype=jnp.int32)
    row = (t // GRID_W).astype(F32)
    col = (t % GRID_W).astype(F32)
    n = HEAD_DIM // 4
    inv = ROPE_THETA ** (-jnp.arange(n, dtype=F32) / n)
    ang = jnp.concatenate([row[:, None] * inv, col[:, None] * inv], axis=-1)
    cos, sin = jnp.cos(ang), jnp.sin(ang)
    cos128 = jnp.concatenate([cos, cos, cos, cos], axis=-1)
    sin128 = jnp.concatenate([-sin, -sin, sin, sin], axis=-1)
    cos128 = jnp.concatenate([cos128, jnp.ones((CTX_LEN, LANES), F32)], axis=0)
    sin128 = jnp.concatenate([sin128, jnp.zeros((CTX_LEN, LANES), F32)], axis=0)
    return cos128, sin128


def _na_bias_tables(na_rpb):
    c = np.arange(GRID_W)[:, None]
    key = np.arange(NA_WIN_ROWS * GRID_W)[None, :]
    i, kc = key // GRID_W, key % GRID_W
    ws = np.clip(c - NA_WIN_COLS // 2, 0, GRID_W - NA_WIN_COLS)
    valid = (kc >= ws) & (kc < ws + NA_WIN_COLS)
    dc = np.clip(kc - c + NA_WIN_COLS - 1, 0, 2 * NA_WIN_COLS - 2)
    d = np.arange(NA_WIN_ROWS)[:, None, None] + i[None]
    d = np.broadcast_to(d, (NA_WIN_ROWS, GRID_W, NA_WIN_ROWS * GRID_W))
    dc = np.broadcast_to(dc[None], d.shape)
    bias = na_rpb[:, :, d, dc]
    return jnp.where(jnp.asarray(valid)[None, None, None], bias, NEG)


def _mod_kernel(c_ref, w_ref, b_ref, o_ref):
    c = c_ref[...]
    act = (c * jax.nn.sigmoid(c)).astype(BF16)
    o_ref[...] = jnp.dot(act, w_ref[...].astype(BF16), preferred_element_type=F32) + b_ref[...]


def _norm_mod(x, gain, shift, scale):
    y = x * lax.rsqrt(jnp.mean(x * x, axis=-1, keepdims=True) + EPS)
    return (y * gain) * (1 + scale) + shift


def _qkv_kernel(x_ref, g_ref, mod_ref, w_ref, cos_ref, sin_ref, qg_ref, kg_ref,
                qb_ref, qa_ref, ka_ref, va_ref, qw_ref, kb_ref, vb_ref, kw_ref, vw_ref):
    h = _norm_mod(x_ref[...], g_ref[...], mod_ref[0:1, :], mod_ref[1:2, :])
    y = jnp.dot(h.astype(BF16), w_ref[...], preferred_element_type=F32)

    lane = lax.broadcasted_iota(jnp.int32, (TM, LANES), 1)
    head_a = ((lane >> 5) & 1) == 0
    cosv, sinv = cos_ref[...], sin_ref[...]
    scale = HEAD_DIM ** -0.5

    def headnorm(xc, gain):
        sq = xc * xc
        sa = jnp.sum(jnp.where(head_a, sq, 0.0), axis=-1, keepdims=True)
        sb = jnp.sum(jnp.where(head_a, 0.0, sq), axis=-1, keepdims=True)
        ms = jnp.where(head_a, sa, sb) * (1.0 / HEAD_DIM)
        return (xc * lax.rsqrt(ms + EPS)) * gain

    def rope(xc):
        return xc * cosv + pltpu.roll(xc, 64, 1) * sinv

    def chunk(c0):
        return y[:, c0:c0 + LANES]

    for c in range(4):
        qb_ref[:, c * LANES:(c + 1) * LANES] = (
            rope(headnorm(chunk(c * LANES), qg_ref[...])) * scale).astype(BF16)
    qa_ref[...] = (y[:, 512:768] * scale).astype(BF16)
    ka_ref[...] = y[:, 768:1024].astype(BF16)
    va_ref[...] = y[:, 1024:1280].astype(BF16)
    for c in range(2):
        qw_ref[:, c * LANES:(c + 1) * LANES] = (rope(chunk(1280 + c * LANES)) * scale).astype(BF16)
    kb_ref[...] = rope(headnorm(chunk(1536), kg_ref[...])).astype(BF16)
    vb_ref[...] = y[:, 1664:1792].astype(BF16)
    kw_ref[...] = rope(chunk(1792)).astype(BF16)
    vw_ref[...] = y[:, 1920:2048].astype(BF16)


def _split_heads(qc, head_a):
    qf = qc.astype(F32)
    return jnp.concatenate([jnp.where(head_a, qf, 0.0), jnp.where(head_a, 0.0, qf)],
                           axis=0).astype(BF16)


def _qk(q, k):
    return lax.dot_general(q, k, (((1,), (1,)), ((), ())), preferred_element_type=F32)


def _merge_heads(o, n):
    lane = lax.broadcasted_iota(jnp.int32, (n, LANES), 1)
    return jnp.where(lane < HEAD_DIM, o[:n], o[n:])


def _global_kernel(q_ref, k_ref, v_ref, o_ref):
    j = pl.program_id(1)
    lane = lax.broadcasted_iota(jnp.int32, (TM, LANES), 1)
    head_a = ((lane >> 5) & 1) == 0

    def run(k, v):
        for c in range(4):
            qs = _split_heads(q_ref[:, c * LANES:(c + 1) * LANES], head_a)
            s = _qk(qs, k)
            m = jnp.max(s, axis=-1, keepdims=True)
            p = jnp.exp(s - m)
            den = jnp.sum(p, axis=-1, keepdims=True)
            o = jnp.dot(p.astype(BF16), v, preferred_element_type=F32) / den
            o_ref[:, c * LANES:(c + 1) * LANES] = _merge_heads(o, TM).astype(BF16)

    @pl.when(j < N_LAT)
    def _():
        run(k_ref[...], v_ref[...])

    @pl.when(j >= N_LAT)
    def _():
        run(k_ref[SEQ:ROWS, :], v_ref[SEQ:ROWS, :])


def _window_kernel(sink_ref, q_ref, k_ref, v_ref, o_ref):
    j = pl.program_id(1)
    lane = lax.broadcasted_iota(jnp.int32, (TM, LANES), 1)
    head_a = ((lane >> 5) & 1) == 0
    span = TM + 2 * WA_RADIUS
    row = lax.broadcasted_iota(jnp.int32, (2 * TM, 1), 0)
    kc, vc = k_ref[SEQ:ROWS, :], v_ref[SEQ:ROWS, :]

    def sink_col(c):
        return jnp.where(row < TM, sink_ref[c], sink_ref[c + 2])

    @pl.when(j < N_LAT)
    def _():
        start = pl.multiple_of(jnp.clip(j * TM - WA_RADIUS, 0, SEQ - span), WA_RADIUS)
        kl, vl = k_ref[pl.ds(start, span), :], v_ref[pl.ds(start, span), :]
        qpos = j * TM + jnp.where(row < TM, row, row - TM)
        kpos = start + lax.broadcasted_iota(jnp.int32, (2 * TM, span), 1)
        valid = jnp.abs(qpos - kpos) <= WA_RADIUS
        for c in range(2):
            qs = _split_heads(q_ref[:, c * LANES:(c + 1) * LANES], head_a)
            s_loc = jnp.where(valid, _qk(qs, kl), NEG)
            s_ctx = _qk(qs, kc)
            sk = sink_col(c)
            m = jnp.maximum(jnp.maximum(jnp.max(s_loc, axis=-1, keepdims=True),
                                        jnp.max(s_ctx, axis=-1, keepdims=True)), sk)
            p_loc, p_ctx = jnp.exp(s_loc - m), jnp.exp(s_ctx - m)
            den = (jnp.sum(p_loc, axis=-1, keepdims=True) + jnp.sum(p_ctx, axis=-1, keepdims=True)
                   + jnp.exp(sk - m))
            o = (jnp.dot(p_loc.astype(BF16), vl, preferred_element_type=F32)
                 + jnp.dot(p_ctx.astype(BF16), vc, preferred_element_type=F32)) / den
            o_ref[:, c * LANES:(c + 1) * LANES] = _merge_heads(o, TM).astype(BF16)

    @pl.when(j >= N_LAT)
    def _():
        for c in range(2):
            qs = _split_heads(q_ref[:, c * LANES:(c + 1) * LANES], head_a)
            s = _qk(qs, kc)
            sk = sink_col(c)
            m = jnp.maximum(jnp.max(s, axis=-1, keepdims=True), sk)
            p = jnp.exp(s - m)
            den = jnp.sum(p, axis=-1, keepdims=True) + jnp.exp(sk - m)
            o = jnp.dot(p.astype(BF16), vc, preferred_element_type=F32) / den
            o_ref[:, c * LANES:(c + 1) * LANES] = _merge_heads(o, TM).astype(BF16)


def _nbr_kernel(q_ref, k_ref, v_ref, bias_ref, o_ref):
    j = pl.program_id(1)
    slab = NA_WIN_ROWS * GRID_W
    rows_per_blk = TM // GRID_W

    @pl.when(j < N_LAT)
    def _():
        lane = lax.broadcasted_iota(jnp.int32, (GRID_W, LANES), 1)
        head_a = lane < HEAD_DIM
        for i in range(rows_per_blk):
            r = j * rows_per_blk + i
            rs = jnp.clip(r - NA_WIN_ROWS // 2, 0, GRID_H - NA_WIN_ROWS)
            d0 = rs - r + (NA_WIN_ROWS - 1)
            start = pl.multiple_of(rs * GRID_W, GRID_W)
            for c in range(2):
                cs = slice(c * LANES, (c + 1) * LANES)
                qs = _split_heads(q_ref[i * GRID_W:(i + 1) * GRID_W, cs], head_a)
                kl, vl = k_ref[pl.ds(start, slab), cs], v_ref[pl.ds(start, slab), cs]
                kc, vc = k_ref[SEQ:ROWS, cs], v_ref[SEQ:ROWS, cs]
                bias = jnp.concatenate([bias_ref[2 * c, d0], bias_ref[2 * c + 1, d0]], axis=0)
                s_loc = _qk(qs, kl) + bias
                s_ctx = _qk(qs, kc)
                m = jnp.maximum(jnp.max(s_loc, axis=-1, keepdims=True),
                                jnp.max(s_ctx, axis=-1, keepdims=True))
                p_loc, p_ctx = jnp.exp(s_loc - m), jnp.exp(s_ctx - m)
                den = jnp.sum(p_loc, axis=-1, keepdims=True) + jnp.sum(p_ctx, axis=-1, keepdims=True)
                o = (jnp.dot(p_loc.astype(BF16), vl, preferred_element_type=F32)
                     + jnp.dot(p_ctx.astype(BF16), vc, preferred_element_type=F32)) / den
                o_ref[i * GRID_W:(i + 1) * GRID_W, cs] = _merge_heads(o, GRID_W).astype(BF16)

    @pl.when(j >= N_LAT)
    def _():
        lane = lax.broadcasted_iota(jnp.int32, (TM, LANES), 1)
        head_a = lane < HEAD_DIM
        for c in range(2):
            cs = slice(c * LANES, (c + 1) * LANES)
            qs = _split_heads(q_ref[:, cs], head_a)
            s = _qk(qs, k_ref[SEQ:ROWS, cs])
            m = jnp.max(s, axis=-1, keepdims=True)
            p = jnp.exp(s - m)
            den = jnp.sum(p, axis=-1, keepdims=True)
            o = jnp.dot(p.astype(BF16), v_ref[SEQ:ROWS, cs], preferred_element_type=F32) / den
            o_ref[:, cs] = _merge_heads(o, TM).astype(BF16)


def _out_proj_kernel(x_ref, oa_ref, ob_ref, ow_ref, w_ref, mod_ref, o_ref):
    o = (jnp.dot(oa_ref[...], w_ref[0:256, :], preferred_element_type=F32)
         + jnp.dot(ob_ref[...], w_ref[256:768, :], preferred_element_type=F32)
         + jnp.dot(ow_ref[...], w_ref[768:1024, :], preferred_element_type=F32))
    o_ref[...] = x_ref[...] + mod_ref[2:3, :] * o


def _ffn_kernel(x_ref, xp_ref, xn_ref, g_ref, mod_ref, wup_ref, cw_ref, cb_ref, wdn_ref, fg_ref,
                o_ref, *, final):
    j = pl.program_id(1)
    xm = x_ref[...]
    xf = jnp.concatenate([xp_ref[...], xm, xn_ref[...]], axis=0)
    h = _norm_mod(xf, g_ref[...], mod_ref[3:4, :], mod_ref[4:5, :])
    ab = jnp.dot(h.astype(BF16), wup_ref[...], preferred_element_type=F32)
    a_mid = ab[HALO:HALO + TM, :D_FF]
    b = ab[HALO:HALO + TM, D_FF:]
    has_prev = jnp.logical_and(j != 0, j != N_LAT)
    has_next = jnp.logical_and(j != N_LAT - 1, j != N_BLK - 1)
    a_prev = jnp.where(has_prev, ab[HALO - 1:HALO, :D_FF], 0.0)
    a_next = jnp.where(has_next, ab[HALO + TM:HALO + TM + 1, :D_FF], 0.0)
    row = lax.broadcasted_iota(jnp.int32, (TM, 1), 0)
    a_up = jnp.where(row == 0, a_prev, pltpu.roll(a_mid, 1, 0))
    a_dn = jnp.where(row == TM - 1, a_next, pltpu.roll(a_mid, TM - 1, 0))
    a = a_up * cw_ref[0:1, :] + a_mid * cw_ref[1:2, :] + a_dn * cw_ref[2:3, :] + cb_ref[...]
    g = (a * jax.nn.sigmoid(a)) * b
    y = jnp.dot(g.astype(BF16), wdn_ref[...], preferred_element_type=F32)
    out = xm + mod_ref[5:6, :] * y
    if final:
        out = (out * lax.rsqrt(jnp.mean(out * out, axis=-1, keepdims=True) + EPS)) * fg_ref[...]
    o_ref[...] = out


def _row_spec(width):
    return pl.BlockSpec((None, TM, width), lambda b, j: (b, j, 0))


def _full_spec(width):
    return pl.BlockSpec((None, ROWS, width), lambda b, j: (b, 0, 0))


def _const_spec(shape):
    nd = len(shape)
    return pl.BlockSpec(shape, lambda b, j: (0,) * nd)


def _layer_spec(l, shape):
    nd = len(shape)
    return pl.BlockSpec((None,) + tuple(shape), lambda b, j: (l,) + (0,) * nd)


def _mod_spec(l):
    return pl.BlockSpec((None, None, N_MOD, D_MODEL),
                        lambda b, j: (l, jnp.where(j < N_LAT, b, BATCH), 0, 0))


def _act_shape(width, dtype=BF16):
    return jax.ShapeDtypeStruct((BATCH, ROWS, width), dtype)


def kernel(x, c, ctx, c_ctx, attn_norm, ffn_norm, w_mod, b_mod, w_in, q_gain, k_gain,
           na_rpb, wa_sink, w_out, w_up, conv_w, conv_b, w_down, final_norm):
    xa = jnp.concatenate([x, ctx], axis=1)
    c_all = jnp.zeros((MOD_ROWS, D_MODEL), F32).at[:BATCH].set(c).at[BATCH].set(c_ctx)
    w_in_p = w_in[:, :, _IN_PERM].astype(BF16)
    w_out_p = w_out[:, _OUT_PERM, :].astype(BF16)
    w_up_b = w_up.astype(BF16)
    w_dn_b = w_down.astype(BF16)
    qg = q_gain[:, _GAIN_PERM].reshape(DEPTH, 1, LANES)
    kg = k_gain[:, _GAIN_PERM].reshape(DEPTH, 1, LANES)
    cos128, sin128 = _rope_tables()
    bias_tab = _na_bias_tables(na_rpb)
    an = attn_norm.reshape(DEPTH, 1, D_MODEL)
    fn = ffn_norm.reshape(DEPTH, 1, D_MODEL)
    fg = final_norm.reshape(1, D_MODEL)
    cb = conv_b.reshape(DEPTH, 1, D_FF)
    bm = b_mod.reshape(DEPTH, 1, N_MOD * D_MODEL)

    n_tiles = N_MOD
    mods = pl.pallas_call(
        _mod_kernel,
        grid=(DEPTH, n_tiles),
        in_specs=[pl.BlockSpec((MOD_ROWS, D_MODEL), lambda l, n: (0, 0)),
                  pl.BlockSpec((None, D_MODEL, D_MODEL), lambda l, n: (l, 0, n)),
                  pl.BlockSpec((None, 1, D_MODEL), lambda l, n: (l, 0, n))],
        out_specs=pl.BlockSpec((None, MOD_ROWS, D_MODEL), lambda l, n: (l, 0, n)),
        out_shape=jax.ShapeDtypeStruct((DEPTH, MOD_ROWS, N_MOD * D_MODEL), F32),
        compiler_params=_params(2),
        name="adaln_mod",
    )(c_all, w_mod, bm)
    mods = mods.reshape(DEPTH, MOD_ROWS, N_MOD, D_MODEL)

    for l in range(DEPTH):
        last = l == DEPTH - 1
        nq = N_LAT if last else N_BLK
        out_rows = SEQ if last else ROWS

        qb, qa, ka, va, qw, kb, vb, kw, vw = pl.pallas_call(
            _qkv_kernel,
            grid=(BATCH, N_BLK),
            in_specs=[_row_spec(D_MODEL), _layer_spec(l, (1, D_MODEL)), _mod_spec(l),
                      _layer_spec(l, (D_MODEL, D_PROJ)),
                      pl.BlockSpec((TM, LANES), lambda b, j: (j, 0)),
                      pl.BlockSpec((TM, LANES), lambda b, j: (j, 0)),
                      _layer_spec(l, (1, LANES)), _layer_spec(l, (1, LANES))],
            out_specs=[_row_spec(512), _row_spec(256), _row_spec(256), _row_spec(256),
                       _row_spec(256), _row_spec(128), _row_spec(128), _row_spec(128),
                       _row_spec(128)],
            out_shape=[_act_shape(512), _act_shape(256), _act_shape(256), _act_shape(256),
                       _act_shape(256), _act_shape(128), _act_shape(128), _act_shape(128),
                       _act_shape(128)],
            compiler_params=_params(2, VMEM_LIMIT),
            name=f"qkv_proj_{l}",
        )(xa, an, mods, w_in_p, cos128, sin128, qg, kg)

        ob = pl.pallas_call(
            _global_kernel,
            grid=(BATCH, nq),
            in_specs=[_row_spec(512), _full_spec(128), _full_spec(128)],
            out_specs=_row_spec(512),
            out_shape=_act_shape(512),
            compiler_params=_params(2, VMEM_LIMIT),
            name=f"global_attn_{l}",
        )(qb, kb, vb)

        ow = pl.pallas_call(
            _window_kernel,
            grid=(BATCH, nq),
            in_specs=[pl.BlockSpec(memory_space=pltpu.SMEM),
                      _row_spec(256), _full_spec(128), _full_spec(128)],
            out_specs=_row_spec(256),
            out_shape=_act_shape(256),
            compiler_params=_params(2, VMEM_LIMIT),
            name=f"window_attn_{l}",
        )(wa_sink[l], qw, kw, vw)

        oa = pl.pallas_call(
            _nbr_kernel,
            grid=(BATCH, nq),
            in_specs=[_row_spec(256), _full_spec(256), _full_spec(256),
                      _layer_spec(l, (NA_HEADS, NA_WIN_ROWS, GRID_W, NA_WIN_ROWS * GRID_W))],
            out_specs=_row_spec(256),
            out_shape=_act_shape(256),
            compiler_params=_params(2, VMEM_LIMIT),
            name=f"nbr_attn_{l}",
        )(qa, ka, va, bias_tab)

        x1 = pl.pallas_call(
            _out_proj_kernel,
            grid=(BATCH, nq),
            in_specs=[_row_spec(D_MODEL), _row_spec(256), _row_spec(512), _row_spec(256),
                      _layer_spec(l, (D_MODEL, D_MODEL)), _mod_spec(l)],
            out_specs=_row_spec(D_MODEL),
            out_shape=jax.ShapeDtypeStruct((BATCH, out_rows, D_MODEL), F32),
            compiler_params=_params(2, VMEM_LIMIT),
            name=f"out_proj_{l}",
        )(xa, oa, ob, ow, w_out_p, mods)

        blk8 = TM // HALO
        n_halo = out_rows // HALO
        xa = pl.pallas_call(
            functools.partial(_ffn_kernel, final=last),
            grid=(BATCH, nq),
            in_specs=[_row_spec(D_MODEL),
                      pl.BlockSpec((None, HALO, D_MODEL),
                                   lambda b, j: (b, jnp.maximum(j * blk8 - 1, 0), 0)),
                      pl.BlockSpec((None, HALO, D_MODEL),
                                   lambda b, j: (b, jnp.minimum((j + 1) * blk8, n_halo - 1), 0)),
                      _layer_spec(l, (1, D_MODEL)), _mod_spec(l),
                      _layer_spec(l, (D_MODEL, 2 * D_FF)),
                      _layer_spec(l, (CONV_W, D_FF)), _layer_spec(l, (1, D_FF)),
                      _layer_spec(l, (D_FF, D_MODEL)), _const_spec((1, D_MODEL))],
            out_specs=_row_spec(D_MODEL),
            out_shape=jax.ShapeDtypeStruct((BATCH, out_rows, D_MODEL), F32),
            compiler_params=_params(2, VMEM_LIMIT),
            name=f"conv_ffn_{l}",
        )(x1, x1, x1, fn, mods, w_up_b, conv_w, cb, w_dn_b, fg)

    return xa
```

```python
import functools
import math

import numpy as np
import jax
import jax.numpy as jnp
from jax import lax
from jax.experimental import pallas as pl
from jax.experimental.pallas import tpu as pltpu

D_MODEL = 1024
BATCH = 8
SEQ = 2048
DEPTH = 4
GRID_W = 64
CTX_LEN = 256
HEAD_DIM = 64
NA_HEADS = 4
GA_HEADS = 8
GA_KV_HEADS = 2
WA_HEADS = 4
WA_KV_HEADS = 2
NA_WIN_ROWS = 8
NA_WIN_COLS = 16
WA_RADIUS = 128
D_FF = 2816
CONV_W = 3
ROPE_THETA = 10000.0
EPS = 1e-6
N_MOD = 6
D_PROJ = 2048

ROWS = SEQ + CTX_LEN
TM = 256
N_BLK = ROWS // TM
N_LAT = SEQ // TM
GRID_H = SEQ // GRID_W
LANES = 128
HALO = 8
MOD_ROWS = 16
NEG = -1e30
LOG2E = math.log2(math.e)
VMEM_LIMIT = 56 * 1024 * 1024

WA_SPAN = TM + 2 * WA_RADIUS
NA_BLK_ROWS = TM // GRID_W
NA_SPAN_ROWS = 12
NA_SPAN = NA_SPAN_ROWS * GRID_W
N_DR = 2 * NA_WIN_ROWS - 1
N_DC = 2 * NA_WIN_COLS - 1
N_PAIR = 16

F32 = jnp.float32
BF16 = jnp.bfloat16


def _params(n_axes, vmem=None):
    return pltpu.CompilerParams(dimension_semantics=("arbitrary",) * n_axes,
                                vmem_limit_bytes=vmem)


def _rope_lane_perm(base, head_stride, n_chunks):
    cols = []
    for c in range(n_chunks):
        for lane in range(LANES):
            part, hsel, i = lane // 64, (lane // 32) % 2, lane % 32
            cols.append(base + HEAD_DIM * (c + head_stride * hsel) + 32 * part + i)
    return cols


def _in_proj_perm():
    qa, ka, va = 0, 256, 512
    qb, kb, vb = 768, 1280, 1408
    qw, kw, vw = 1536, 1792, 1920
    cols = []
    cols += _rope_lane_perm(qb, 4, 4)
    cols += list(range(qa, qa + 256))
    cols += list(range(ka, ka + 256))
    cols += list(range(va, va + 256))
    cols += _rope_lane_perm(qw, 2, 2)
    cols += _rope_lane_perm(kb, 1, 1)
    cols += list(range(vb, vb + 128))
    cols += _rope_lane_perm(kw, 1, 1)
    cols += list(range(vw, vw + 128))
    return np.asarray(cols, np.int32)


def _out_proj_perm():
    rows = list(range(256))
    for c in range(4):
        for lane in range(LANES):
            rows.append(256 + HEAD_DIM * (c + 4 * (lane // 64)) + lane % 64)
    for c in range(2):
        for lane in range(LANES):
            rows.append(768 + HEAD_DIM * (c + 2 * (lane // 64)) + lane % 64)
    return np.asarray(rows, np.int32)


_IN_PERM = _in_proj_perm()
_OUT_PERM = _out_proj_perm()
_GAIN_PERM = np.asarray([32 * (l // 64) + l % 32 for l in range(LANES)], np.int32)
_PAIR_ROW = np.clip(np.arange(N_PAIR + 1) - 1, 0, N_DR - 1)


def _rope_tables():
    t = jnp.arange(SEQ, dtype=jnp.int32)
    row = (t // GRID_W).astype(F32)
    col = (t % GRID_W).astype(F32)
    n = HEAD_DIM // 4
    inv = ROPE_THETA ** (-jnp.arange(n, dtype=F32) / n)
    ang = jnp.concatenate([row[:, None] * inv, col[:, None] * inv], axis=-1)
    cos, sin = jnp.cos(ang), jnp.sin(ang)
    cos128 = jnp.concatenate([cos, cos, cos, cos], axis=-1)
    sin128 = jnp.concatenate([-sin, -sin, sin, sin], axis=-1)
    cos128 = jnp.concatenate([cos128, jnp.ones((CTX_LEN, LANES), F32)], axis=0)
    sin128 = jnp.concatenate([sin128, jnp.zeros((CTX_LEN, LANES), F32)], axis=0)
    return cos128, sin128


def _na_bias_rows(na_rpb):
    rp = jnp.pad(na_rpb, ((0, 0), (0, 0), (0, 0), (0, GRID_W - N_DC)))
    return jnp.concatenate([rp[:, :, _PAIR_ROW[:-1]], rp[:, :, _PAIR_ROW[1:]]], axis=-1)


def _na_bias_kernel(t_ref, o_ref):
    t = t_ref[...] * LOG2E
    kc = lax.broadcasted_iota(jnp.int32, (N_PAIR, LANES), 1) & (GRID_W - 1)
    for c in range(GRID_W):
        ws = min(max(c - NA_WIN_COLS // 2, 0), GRID_W - NA_WIN_COLS)
        rolled = pltpu.roll(t, (LANES - (NA_WIN_COLS - 1) + c) % LANES, 1)
        o_ref[:, c, :] = jnp.where(jnp.logical_and(kc >= ws, kc < ws + NA_WIN_COLS), rolled, NEG)


def _mod_kernel(c_ref, w_ref, b_ref, o_ref):
    c = c_ref[...]
    act = (c * jax.nn.sigmoid(c)).astype(BF16)
    o_ref[...] = jnp.dot(act, w_ref[...].astype(BF16), preferred_element_type=F32) + b_ref[...]


def _norm_mod(x, gain, shift, scale):
    y = x * lax.rsqrt(jnp.mean(x * x, axis=-1, keepdims=True) + EPS)
    return (y * gain) * (1 + scale) + shift


def _qkv_kernel(x_ref, g_ref, mod_ref, w_ref, cos_ref, sin_ref, qg_ref, kg_ref,
                qb_ref, qa_ref, qw_ref, kb_ref, vba_ref, vbb_ref, kw_ref, vwa_ref, vwb_ref,
                ka_ref, vaa_ref, vab_ref):
    h = _norm_mod(x_ref[...], g_ref[...], mod_ref[0:1, :], mod_ref[1:2, :])
    y = jnp.dot(h.astype(BF16), w_ref[...], preferred_element_type=F32)

    lane = lax.broadcasted_iota(jnp.int32, (TM, LANES), 1)
    head_a = ((lane >> 5) & 1) == 0
    lane_lo = lane < HEAD_DIM
    cosv, sinv = cos_ref[...], sin_ref[...]
    qscale = HEAD_DIM ** -0.5 * LOG2E

    def headnorm(xc, gain):
        sq = xc * xc
        sa = jnp.sum(jnp.where(head_a, sq, 0.0), axis=-1, keepdims=True)
        sb = jnp.sum(jnp.where(head_a, 0.0, sq), axis=-1, keepdims=True)
        ms = jnp.where(head_a, sa, sb) * (1.0 / HEAD_DIM)
        return (xc * lax.rsqrt(ms + EPS)) * gain

    def rope(xc):
        return xc * cosv + pltpu.roll(xc, 64, 1) * sinv

    def chunk(c0):
        return y[:, c0:c0 + LANES]

    def put_v(va_ref, vb_ref, c, vc):
        va_ref[:, c * LANES:(c + 1) * LANES] = jnp.where(lane_lo, vc, 1.0).astype(BF16)
        vb_ref[:, c * LANES:(c + 1) * LANES] = jnp.where(lane_lo, 1.0, vc).astype(BF16)

    for c in range(4):
        qb_ref[:, c * LANES:(c + 1) * LANES] = (
            rope(headnorm(chunk(c * LANES), qg_ref[...])) * qscale).astype(BF16)
    qa_ref[...] = (y[:, 512:768] * qscale).astype(BF16)
    ka_ref[...] = y[:, 768:1024].astype(BF16)
    for c in range(2):
        put_v(vaa_ref, vab_ref, c, chunk(1024 + c * LANES))
        qw_ref[:, c * LANES:(c + 1) * LANES] = (rope(chunk(1280 + c * LANES)) * qscale).astype(BF16)
    kb_ref[...] = rope(headnorm(chunk(1536), kg_ref[...])).astype(BF16)
    put_v(vba_ref, vbb_ref, 0, chunk(1664))
    kw_ref[...] = rope(chunk(1792)).astype(BF16)
    put_v(vwa_ref, vwb_ref, 0, chunk(1920))


def _qk(q, k):
    return lax.dot_general(q, k, (((1,), (1,)), ((), ())), preferred_element_type=F32)


def _softmax_pv(parts, extra=None):
    m = None
    for s, _ in parts:
        sm = jnp.max(s, axis=-1, keepdims=True)
        m = sm if m is None else jnp.maximum(m, sm)
    if extra is not None:
        m = jnp.maximum(m, extra)
    o = None
    for s, v in parts:
        pv = jnp.dot(jnp.exp2(s - m).astype(BF16), v, preferred_element_type=F32)
        o = pv if o is None else o + pv
    den = pltpu.roll(o, HEAD_DIM, 1)
    if extra is not None:
        den = den + jnp.exp2(extra - m)
    return o / den


def _attn_kernel(sink_ref, x_ref, mod_ref, wo_ref, bias_ref, qb_ref, qa_ref, qw_ref,
                 kb_ref, vba_ref, vbb_ref, kw_ref, vwa_ref, vwb_ref, ka_ref, vaa_ref, vab_ref,
                 o_ref, att_ref):
    j = pl.program_id(1)
    lane = lax.broadcasted_iota(jnp.int32, (TM, LANES), 1)
    rope_a = ((lane >> 5) & 1) == 0
    lane_lo = lane < HEAD_DIM
    ctx = slice(SEQ, ROWS)

    def heads(q_ref, c, a_mask):
        qf = q_ref[:, c * LANES:(c + 1) * LANES].astype(F32)
        return jnp.where(a_mask, qf, 0.0).astype(BF16), jnp.where(a_mask, 0.0, qf).astype(BF16)

    def put(col, o_a, o_b):
        att_ref[:, col:col + LANES] = jnp.where(lane_lo, o_a, o_b).astype(BF16)

    @pl.when(j < N_LAT)
    def _():
        kb, vba, vbb = kb_ref[...], vba_ref[...], vbb_ref[...]
        for c in range(4):
            q_a, q_b = heads(qb_ref, c, rope_a)
            put(256 + c * LANES,
                _softmax_pv([(_qk(q_a, kb), vba)]), _softmax_pv([(_qk(q_b, kb), vbb)]))

        start = pl.multiple_of(jnp.clip(j * TM - WA_RADIUS, 0, SEQ - WA_SPAN), WA_RADIUS)
        loc = pl.ds(start, WA_SPAN)
        qpos = j * TM + lax.broadcasted_iota(jnp.int32, (TM, 1), 0)
        kpos = start + lax.broadcasted_iota(jnp.int32, (1, WA_SPAN), 1)
        near = jnp.abs(qpos - kpos) <= WA_RADIUS
        kl, kc = kw_ref[loc, :], kw_ref[ctx, :]
        for c in range(2):
            outs = []
            for q_h, v_ref, h in zip(heads(qw_ref, c, rope_a), (vwa_ref, vwb_ref), (c, c + 2)):
                outs.append(_softmax_pv([(jnp.where(near, _qk(q_h, kl), NEG), v_ref[loc, :]),
                                         (_qk(q_h, kc), v_ref[ctx, :])],
                                        extra=sink_ref[h] * LOG2E))
            put(768 + c * LANES, *outs)

        row0 = j * NA_BLK_ROWS
        span0 = jnp.clip(row0 - NA_WIN_ROWS // 2, 0, GRID_H - NA_SPAN_ROWS)
        nloc = pl.ds(pl.multiple_of(span0 * GRID_W, GRID_W), NA_SPAN)
        qrow = row0 + (lax.broadcasted_iota(jnp.int32, (TM, 1), 0) >> 6)
        wrow = jnp.clip(qrow - NA_WIN_ROWS // 2, 0, GRID_H - NA_WIN_ROWS)
        krow = span0 + (lax.broadcasted_iota(jnp.int32, (1, NA_SPAN), 1) >> 6)
        in_rows = jnp.logical_and(krow >= wrow, krow < wrow + NA_WIN_ROWS)
        for c in range(2):
            cs = slice(c * LANES, (c + 1) * LANES)
            kl, kc = ka_ref[nloc, cs], ka_ref[ctx, cs]
            outs = []
            for q_h, v_ref, h in zip(heads(qa_ref, c, lane_lo), (vaa_ref, vab_ref), (2 * c, 2 * c + 1)):
                bias = jnp.concatenate(
                    [jnp.concatenate(
                        [bias_ref[h, jnp.clip(span0 + 2 * p - (row0 + iq) + NA_WIN_ROWS, 0, N_PAIR - 1)]
                         for p in range(NA_SPAN_ROWS // 2)], axis=1)
                     for iq in range(NA_BLK_ROWS)], axis=0)
                s_loc = jnp.where(in_rows, _qk(q_h, kl) + bias, NEG)
                outs.append(_softmax_pv([(s_loc, v_ref[nloc, cs]), (_qk(q_h, kc), v_ref[ctx, cs])]))
            put(c * LANES, *outs)

    @pl.when(j >= N_LAT)
    def _():
        kb = kb_ref[ctx, :]
        for c in range(4):
            q_a, q_b = heads(qb_ref, c, rope_a)
            put(256 + c * LANES, _softmax_pv([(_qk(q_a, kb), vba_ref[ctx, :])]),
                _softmax_pv([(_qk(q_b, kb), vbb_ref[ctx, :])]))
        kc = kw_ref[ctx, :]
        for c in range(2):
            outs = [_softmax_pv([(_qk(q_h, kc), v_ref[ctx, :])], extra=sink_ref[h] * LOG2E)
                    for q_h, v_ref, h in zip(heads(qw_ref, c, rope_a), (vwa_ref, vwb_ref), (c, c + 2))]
            put(768 + c * LANES, *outs)
        for c in range(2):
            cs = slice(c * LANES, (c + 1) * LANES)
            outs = [_softmax_pv([(_qk(q_h, ka_ref[ctx, cs]), v_ref[ctx, cs])])
                    for q_h, v_ref in zip(heads(qa_ref, c, lane_lo), (vaa_ref, vab_ref))]
            put(c * LANES, *outs)

    o = jnp.dot(att_ref[...], wo_ref[...], preferred_element_type=F32)
    o_ref[...] = x_ref[...] + mod_ref[2:3, :] * o


def _ffn_kernel(x_ref, xp_ref, xn_ref, g_ref, mod_ref, wup_ref, cw_ref, cb_ref, wdn_ref, fg_ref,
                o_ref, *, final):
    j = pl.program_id(1)
    xm = x_ref[...]
    xf = jnp.concatenate([xp_ref[...], xm, xn_ref[...]], axis=0)
    h = _norm_mod(xf, g_ref[...], mod_ref[3:4, :], mod_ref[4:5, :])
    ab = jnp.dot(h.astype(BF16), wup_ref[...], preferred_element_type=F32)
    a_mid = ab[HALO:HALO + TM, :D_FF]
    b = ab[HALO:HALO + TM, D_FF:]
    has_prev = jnp.logical_and(j != 0, j != N_LAT)
    has_next = jnp.logical_and(j != N_LAT - 1, j != N_BLK - 1)
    a_prev = jnp.where(has_prev, ab[HALO - 1:HALO, :D_FF], 0.0)
    a_next = jnp.where(has_next, ab[HALO + TM:HALO + TM + 1, :D_FF], 0.0)
    row = lax.broadcasted_iota(jnp.int32, (TM, 1), 0)
    a_up = jnp.where(row == 0, a_prev, pltpu.roll(a_mid, 1, 0))
    a_dn = jnp.where(row == TM - 1, a_next, pltpu.roll(a_mid, TM - 1, 0))
    a = a_up * cw_ref[0:1, :] + a_mid * cw_ref[1:2, :] + a_dn * cw_ref[2:3, :] + cb_ref[...]
    g = (a * jax.nn.sigmoid(a)) * b
    y = jnp.dot(g.astype(BF16), wdn_ref[...], preferred_element_type=F32)
    out = xm + mod_ref[5:6, :] * y
    if final:
        out = (out * lax.rsqrt(jnp.mean(out * out, axis=-1, keepdims=True) + EPS)) * fg_ref[...]
    o_ref[...] = out


def _row_spec(width):
    return pl.BlockSpec((None, TM, width), lambda b, j: (b, j, 0))


def _full_spec(width):
    return pl.BlockSpec((None, ROWS, width), lambda b, j: (b, 0, 0))


def _const_spec(shape):
    nd = len(shape)
    return pl.BlockSpec(shape, lambda b, j: (0,) * nd)


def _layer_spec(l, shape):
    nd = len(shape)
    return pl.BlockSpec((None,) + tuple(shape), lambda b, j: (l,) + (0,) * nd)


def _mod_spec(l):
    return pl.BlockSpec((None, None, N_MOD, D_MODEL),
                        lambda b, j: (l, jnp.where(j < N_LAT, b, BATCH), 0, 0))


def _act_shape(width, dtype=BF16):
    return jax.ShapeDtypeStruct((BATCH, ROWS, width), dtype)


def kernel(x, c, ctx, c_ctx, attn_norm, ffn_norm, w_mod, b_mod, w_in, q_gain, k_gain,
           na_rpb, wa_sink, w_out, w_up, conv_w, conv_b, w_down, final_norm):
    xa = jnp.concatenate([x, ctx], axis=1)
    c_all = jnp.zeros((MOD_ROWS, D_MODEL), F32).at[:BATCH].set(c).at[BATCH].set(c_ctx)
    w_in_p = w_in[:, :, _IN_PERM].astype(BF16)
    w_out_p = w_out[:, _OUT_PERM, :].astype(BF16)
    w_up_b = w_up.astype(BF16)
    w_dn_b = w_down.astype(BF16)
    qg = q_gain[:, _GAIN_PERM].reshape(DEPTH, 1, LANES)
    kg = k_gain[:, _GAIN_PERM].reshape(DEPTH, 1, LANES)
    cos128, sin128 = _rope_tables()
    an = attn_norm.reshape(DEPTH, 1, D_MODEL)
    fn = ffn_norm.reshape(DEPTH, 1, D_MODEL)
    fg = final_norm.reshape(1, D_MODEL)
    cb = conv_b.reshape(DEPTH, 1, D_FF)
    bm = b_mod.reshape(DEPTH, 1, N_MOD * D_MODEL)

    bias_tab = pl.pallas_call(
        _na_bias_kernel,
        grid=(DEPTH, NA_HEADS),
        in_specs=[pl.BlockSpec((None, None, N_PAIR, LANES), lambda l, h: (l, h, 0, 0))],
        out_specs=pl.BlockSpec((None, None, N_PAIR, GRID_W, LANES), lambda l, h: (l, h, 0, 0, 0)),
        out_shape=jax.ShapeDtypeStruct((DEPTH, NA_HEADS, N_PAIR, GRID_W, LANES), F32),
        compiler_params=_params(2),
        name="nbr_bias",
    )(_na_bias_rows(na_rpb))

    mods = pl.pallas_call(
        _mod_kernel,
        grid=(DEPTH, N_MOD),
        in_specs=[pl.BlockSpec((MOD_ROWS, D_MODEL), lambda l, n: (0, 0)),
                  pl.BlockSpec((None, D_MODEL, D_MODEL), lambda l, n: (l, 0, n)),
                  pl.BlockSpec((None, 1, D_MODEL), lambda l, n: (l, 0, n))],
        out_specs=pl.BlockSpec((None, MOD_ROWS, D_MODEL), lambda l, n: (l, 0, n)),
        out_shape=jax.ShapeDtypeStruct((DEPTH, MOD_ROWS, N_MOD * D_MODEL), F32),
        compiler_params=_params(2),
        name="adaln_mod",
    )(c_all, w_mod, bm)
    mods = mods.reshape(DEPTH, MOD_ROWS, N_MOD, D_MODEL)

    qkv_widths = (512, 256, 256, 128, 128, 128, 128, 128, 128, 256, 256, 256)
    for l in range(DEPTH):
        last = l == DEPTH - 1
        nq = N_LAT if last else N_BLK
        out_rows = SEQ if last else ROWS

        qkv = pl.pallas_call(
            _qkv_kernel,
            grid=(BATCH, N_BLK),
            in_specs=[_row_spec(D_MODEL), _layer_spec(l, (1, D_MODEL)), _mod_spec(l),
                      _layer_spec(l, (D_MODEL, D_PROJ)),
                      pl.BlockSpec((TM, LANES), lambda b, j: (j, 0)),
                      pl.BlockSpec((TM, LANES), lambda b, j: (j, 0)),
                      _layer_spec(l, (1, LANES)), _layer_spec(l, (1, LANES))],
            out_specs=[_row_spec(w) for w in qkv_widths],
            out_shape=[_act_shape(w) for w in qkv_widths],
            compiler_params=_params(2, VMEM_LIMIT),
            name=f"qkv_proj_{l}",
        )(xa, an, mods, w_in_p, cos128, sin128, qg, kg)

        x1 = pl.pallas_call(
            _attn_kernel,
            grid=(BATCH, nq),
            in_specs=[pl.BlockSpec(memory_space=pltpu.SMEM),
                      _row_spec(D_MODEL), _mod_spec(l), _layer_spec(l, (D_MODEL, D_MODEL)),
                      _layer_spec(l, (NA_HEADS, N_PAIR, GRID_W, LANES)),
                      _row_spec(512), _row_spec(256), _row_spec(256)]
                     + [_full_spec(w) for w in qkv_widths[3:]],
            out_specs=_row_spec(D_MODEL),
            out_shape=jax.ShapeDtypeStruct((BATCH, out_rows, D_MODEL), F32),
            scratch_shapes=[pltpu.VMEM((TM, D_MODEL), BF16)],
            compiler_params=_params(2, VMEM_LIMIT),
            name=f"attn_{l}",
        )(wa_sink[l], xa, mods, w_out_p, bias_tab, *qkv)

        blk8 = TM // HALO
        n_halo = out_rows // HALO
        xa = pl.pallas_call(
            functools.partial(_ffn_kernel, final=last),
            grid=(BATCH, nq),
            in_specs=[_row_spec(D_MODEL),
                      pl.BlockSpec((None, HALO, D_MODEL),
                                   lambda b, j: (b, jnp.maximum(j * blk8 - 1, 0), 0)),
                      pl.BlockSpec((None, HALO, D_MODEL),
                                   lambda b, j: (b, jnp.minimum((j + 1) * blk8, n_halo - 1), 0)),
                      _layer_spec(l, (1, D_MODEL)), _mod_spec(l),
                      _layer_spec(l, (D_MODEL, 2 * D_FF)),
                      _layer_spec(l, (CONV_W, D_FF)), _layer_spec(l, (1, D_FF)),
                      _layer_spec(l, (D_FF, D_MODEL)), _const_spec((1, D_MODEL))],
            out_specs=_row_spec(D_MODEL),
            out_shape=jax.ShapeDtypeStruct((BATCH, out_rows, D_MODEL), F32),
            compiler_params=_params(2, VMEM_LIMIT),
            name=f"conv_ffn_{l}",
        )(x1, x1, x1, fn, mods, w_up_b, conv_w, cb, w_dn_b, fg)

    return xa
```

```python
import functools
import math

import numpy as np
import jax
import jax.numpy as jnp
from jax import lax
from jax.experimental import pallas as pl
from jax.experimental.pallas import tpu as pltpu

D_MODEL = 1024
BATCH = 8
SEQ = 2048
DEPTH = 4
GRID_W = 64
CTX_LEN = 256
HEAD_DIM = 64
NA_HEADS = 4
GA_HEADS = 8
GA_KV_HEADS = 2
WA_HEADS = 4
WA_KV_HEADS = 2
NA_WIN_ROWS = 8
NA_WIN_COLS = 16
WA_RADIUS = 128
D_FF = 2816
CONV_W = 3
ROPE_THETA = 10000.0
EPS = 1e-6
N_MOD = 6
D_PROJ = 2048

ROWS = SEQ + CTX_LEN
TM = 256
N_BLK = ROWS // TM
N_LAT = SEQ // TM
GRID_H = SEQ // GRID_W
LANES = 128
HALO = 8
MOD_ROWS = 16
NEG = -1e30
LOG2E = math.log2(math.e)
VMEM_LIMIT = 56 * 1024 * 1024

WA_SPAN = TM + 2 * WA_RADIUS
NA_BLK_ROWS = TM // GRID_W
NA_SPAN_ROWS = 12
NA_SPAN = NA_SPAN_ROWS * GRID_W
N_DR = 2 * NA_WIN_ROWS - 1
N_DC = 2 * NA_WIN_COLS - 1
N_PAIR = 16
ATTN_AHEAD = 1
QKV_SUB = 2

F32 = jnp.float32
BF16 = jnp.bfloat16


def _params(n_axes, vmem=None):
    return pltpu.CompilerParams(dimension_semantics=("arbitrary",) * n_axes,
                                vmem_limit_bytes=vmem)


def _rope_lane_perm(base, head_stride, n_chunks):
    cols = []
    for c in range(n_chunks):
        for lane in range(LANES):
            part, hsel, i = lane // 64, (lane // 32) % 2, lane % 32
            cols.append(base + HEAD_DIM * (c + head_stride * hsel) + 32 * part + i)
    return cols


def _in_proj_perm():
    qa, ka, va = 0, 256, 512
    qb, kb, vb = 768, 1280, 1408
    qw, kw, vw = 1536, 1792, 1920
    cols = []
    cols += _rope_lane_perm(qb, 4, 4)
    cols += list(range(qa, qa + 256))
    cols += list(range(ka, ka + 256))
    cols += list(range(va, va + 256))
    cols += _rope_lane_perm(qw, 2, 2)
    cols += _rope_lane_perm(kb, 1, 1)
    cols += list(range(vb, vb + 128))
    cols += _rope_lane_perm(kw, 1, 1)
    cols += list(range(vw, vw + 128))
    return np.asarray(cols, np.int32)


def _out_proj_perm():
    rows = list(range(256))
    for c in range(4):
        for lane in range(LANES):
            rows.append(256 + HEAD_DIM * (c + 4 * (lane // 64)) + lane % 64)
    for c in range(2):
        for lane in range(LANES):
            rows.append(768 + HEAD_DIM * (c + 2 * (lane // 64)) + lane % 64)
    return np.asarray(rows, np.int32)


_IN_PERM = _in_proj_perm()
_OUT_PERM = _out_proj_perm()
_GAIN_PERM = np.asarray([32 * (l // 64) + l % 32 for l in range(LANES)], np.int32)
_PAIR_ROW = np.clip(np.arange(N_PAIR + 1) - 1, 0, N_DR - 1)


def _take_runs(a, perm, axis):
    cuts = [0] + [i for i in range(1, len(perm)) if perm[i] != perm[i - 1] + 1] + [len(perm)]
    return jnp.concatenate(
        [lax.slice_in_dim(a, int(perm[s]), int(perm[e - 1]) + 1, axis=axis)
         for s, e in zip(cuts[:-1], cuts[1:])], axis=axis)


def _rope_tables():
    t = jnp.arange(SEQ, dtype=jnp.int32)
    row = (t // GRID_W).astype(F32)
    col = (t % GRID_W).astype(F32)
    n = HEAD_DIM // 4
    inv = ROPE_THETA ** (-jnp.arange(n, dtype=F32) / n)
    ang = jnp.concatenate([row[:, None] * inv, col[:, None] * inv], axis=-1)
    cos, sin = jnp.cos(ang), jnp.sin(ang)
    cos128 = jnp.concatenate([cos, cos, cos, cos], axis=-1)
    sin128 = jnp.concatenate([-sin, -sin, sin, sin], axis=-1)
    cos128 = jnp.concatenate([cos128, jnp.ones((CTX_LEN, LANES), F32)], axis=0)
    sin128 = jnp.concatenate([sin128, jnp.zeros((CTX_LEN, LANES), F32)], axis=0)
    return cos128, sin128


def _na_bias_rows(na_rpb):
    rp = jnp.pad(na_rpb, ((0, 0), (0, 0), (0, 0), (0, GRID_W - N_DC)))
    return jnp.concatenate([rp[:, :, _PAIR_ROW[:-1]], rp[:, :, _PAIR_ROW[1:]]], axis=-1)


def _na_bias_kernel(t_ref, o_ref):
    t = t_ref[...] * LOG2E
    kc = lax.broadcasted_iota(jnp.int32, (N_PAIR, LANES), 1) & (GRID_W - 1)
    for c in range(GRID_W):
        ws = min(max(c - NA_WIN_COLS // 2, 0), GRID_W - NA_WIN_COLS)
        rolled = pltpu.roll(t, (LANES - (NA_WIN_COLS - 1) + c) % LANES, 1)
        o_ref[:, c, :] = jnp.where(jnp.logical_and(kc >= ws, kc < ws + NA_WIN_COLS), rolled, NEG)


def _mod_kernel(c_ref, w_ref, b_ref, o_ref):
    c = c_ref[...]
    act = (c * jax.nn.sigmoid(c)).astype(BF16)
    o_ref[...] = jnp.dot(act, w_ref[...].astype(BF16), preferred_element_type=F32) + b_ref[...]


def _norm_mod(x, gain, shift, scale):
    y = x * lax.rsqrt(jnp.mean(x * x, axis=-1, keepdims=True) + EPS)
    return (y * gain) * (1 + scale) + shift


def _qkv_kernel(x_ref, g_ref, mod_ref, w_ref, cos_ref, sin_ref, qg_ref, kg_ref,
                qb_ref, qa_ref, qw_ref, kb_ref, vba_ref, vbb_ref, kw_ref, vwa_ref, vwb_ref,
                ka_ref, vaa_ref, vab_ref):
    sub = TM // QKV_SUB
    lane = lax.broadcasted_iota(jnp.int32, (sub, LANES), 1)
    head_a = ((lane >> 5) & 1) == 0
    lane_lo = lane < HEAD_DIM
    qscale = HEAD_DIM ** -0.5 * LOG2E

    def headnorm(xc, gain):
        sq = xc * xc
        sa = jnp.sum(jnp.where(head_a, sq, 0.0), axis=-1, keepdims=True)
        sb = jnp.sum(jnp.where(head_a, 0.0, sq), axis=-1, keepdims=True)
        ms = jnp.where(head_a, sa, sb) * (1.0 / HEAD_DIM)
        return (xc * lax.rsqrt(ms + EPS)) * gain

    def epilogue(y, rows):
        cosv, sinv = cos_ref[rows, :], sin_ref[rows, :]

        def rope(xc):
            return xc * cosv + pltpu.roll(xc, 64, 1) * sinv

        def chunk(c0):
            return y[:, c0:c0 + LANES]

        def put_v(va_ref, vb_ref, c, vc):
            va_ref[rows, c * LANES:(c + 1) * LANES] = jnp.where(lane_lo, vc, 1.0).astype(BF16)
            vb_ref[rows, c * LANES:(c + 1) * LANES] = jnp.where(lane_lo, 1.0, vc).astype(BF16)

        for c in range(4):
            qb_ref[rows, c * LANES:(c + 1) * LANES] = (
                rope(headnorm(chunk(c * LANES), qg_ref[...])) * qscale).astype(BF16)
        qa_ref[rows, :] = (y[:, 512:768] * qscale).astype(BF16)
        ka_ref[rows, :] = y[:, 768:1024].astype(BF16)
        for c in range(2):
            put_v(vaa_ref, vab_ref, c, chunk(1024 + c * LANES))
            qw_ref[rows, c * LANES:(c + 1) * LANES] = (
                rope(chunk(1280 + c * LANES)) * qscale).astype(BF16)
        kb_ref[rows, :] = rope(headnorm(chunk(1536), kg_ref[...])).astype(BF16)
        put_v(vba_ref, vbb_ref, 0, chunk(1664))
        kw_ref[rows, :] = rope(chunk(1792)).astype(BF16)
        put_v(vwa_ref, vwb_ref, 0, chunk(1920))

    ys = []
    for s in range(QKV_SUB):
        rows = slice(s * sub, (s + 1) * sub)
        h = _norm_mod(x_ref[rows, :], g_ref[...], mod_ref[0:1, :], mod_ref[1:2, :])
        ys.append((jnp.dot(h.astype(BF16), w_ref[...], preferred_element_type=F32), rows))
    for y, rows in ys:
        epilogue(y, rows)


def _qk(q, k):
    return lax.dot_general(q, k, (((1,), (1,)), ((), ())), preferred_element_type=F32)


def _softmax_pv(parts, extra=None):
    m = None
    for s, _ in parts:
        sm = jnp.max(s, axis=-1, keepdims=True)
        m = sm if m is None else jnp.maximum(m, sm)
    if extra is not None:
        m = jnp.maximum(m, extra)
    o = None
    for s, v in parts:
        pv = jnp.dot(jnp.exp2(s - m).astype(BF16), v, preferred_element_type=F32)
        o = pv if o is None else o + pv
    den = pltpu.roll(o, HEAD_DIM, 1)
    if extra is not None:
        den = den + jnp.exp2(extra - m)
    return o / den


def _attn_kernel(sink_ref, x_ref, mod_ref, wo_ref, bias_ref, qb_ref, qa_ref, qw_ref,
                 kb_ref, vba_ref, vbb_ref, kw_ref, vwa_ref, vwb_ref, ka_ref, vaa_ref, vab_ref,
                 o_ref, att_ref):
    j = pl.program_id(1)
    lane = lax.broadcasted_iota(jnp.int32, (TM, LANES), 1)
    rope_a = ((lane >> 5) & 1) == 0
    lane_lo = lane < HEAD_DIM
    ctx = slice(SEQ, ROWS)

    def heads(q_ref, c, a_mask):
        qf = q_ref[:, c * LANES:(c + 1) * LANES].astype(F32)
        return jnp.where(a_mask, qf, 0.0).astype(BF16), jnp.where(a_mask, 0.0, qf).astype(BF16)

    def run(tasks):
        held = {}
        queue = [task[2]() for task in tasks[:ATTN_AHEAD]]
        for t, (col, which, _) in enumerate(tasks):
            if t + ATTN_AHEAD < len(tasks):
                queue.append(tasks[t + ATTN_AHEAD][2]())
            o = _softmax_pv(*queue.pop(0))
            if which == 0:
                held[col] = o
            else:
                att_ref[:, col:col + LANES] = jnp.where(lane_lo, held.pop(col), o).astype(BF16)

    def head_tasks(col, q_ref, c, a_mask, score):
        return [(col, w, functools.partial(lambda w: score(heads(q_ref, c, a_mask)[w], w), w))
                for w in range(2)]

    @pl.when(j < N_LAT)
    def _():
        tasks = []
        vb_refs = (vba_ref, vbb_ref)
        for c in range(4):
            tasks += head_tasks(256 + c * LANES, qb_ref, c, rope_a,
                                lambda q_h, w: ([(_qk(q_h, kb_ref[...]), vb_refs[w][...])], None))

        start = pl.multiple_of(jnp.clip(j * TM - WA_RADIUS, 0, SEQ - WA_SPAN), WA_RADIUS)
        loc = pl.ds(start, WA_SPAN)
        qpos = j * TM + lax.broadcasted_iota(jnp.int32, (TM, 1), 0)
        kpos = start + lax.broadcasted_iota(jnp.int32, (1, WA_SPAN), 1)
        near = jnp.abs(qpos - kpos) <= WA_RADIUS
        vw_refs = (vwa_ref, vwb_ref)
        for c in range(2):
            def w_score(q_h, w, c=c):
                return ([(jnp.where(near, _qk(q_h, kw_ref[loc, :]), NEG), vw_refs[w][loc, :]),
                         (_qk(q_h, kw_ref[ctx, :]), vw_refs[w][ctx, :])],
                        sink_ref[c + 2 * w] * LOG2E)
            tasks += head_tasks(768 + c * LANES, qw_ref, c, rope_a, w_score)

        row0 = j * NA_BLK_ROWS
        span0 = jnp.clip(row0 - NA_WIN_ROWS // 2, 0, GRID_H - NA_SPAN_ROWS)
        nloc = pl.ds(pl.multiple_of(span0 * GRID_W, GRID_W), NA_SPAN)
        qrow = row0 + (lax.broadcasted_iota(jnp.int32, (TM, 1), 0) >> 6)
        wrow = jnp.clip(qrow - NA_WIN_ROWS // 2, 0, GRID_H - NA_WIN_ROWS)
        krow = span0 + (lax.broadcasted_iota(jnp.int32, (1, NA_SPAN), 1) >> 6)
        in_rows = jnp.logical_and(krow >= wrow, krow < wrow + NA_WIN_ROWS)
        va_refs = (vaa_ref, vab_ref)
        for c in range(2):
            def a_score(q_h, w, c=c):
                cs = slice(c * LANES, (c + 1) * LANES)
                bias = jnp.concatenate(
                    [jnp.concatenate(
                        [bias_ref[2 * c + w,
                                  jnp.clip(span0 + 2 * p - (row0 + iq) + NA_WIN_ROWS, 0, N_PAIR - 1)]
                         for p in range(NA_SPAN_ROWS // 2)], axis=1)
                     for iq in range(NA_BLK_ROWS)], axis=0)
                s_loc = jnp.where(in_rows, _qk(q_h, ka_ref[nloc, cs]) + bias, NEG)
                return ([(s_loc, va_refs[w][nloc, cs]),
                         (_qk(q_h, ka_ref[ctx, cs]), va_refs[w][ctx, cs])], None)
            tasks += head_tasks(c * LANES, qa_ref, c, lane_lo, a_score)
        run(tasks)

    @pl.when(j >= N_LAT)
    def _():
        tasks = []
        vb_refs, vw_refs, va_refs = (vba_ref, vbb_ref), (vwa_ref, vwb_ref), (vaa_ref, vab_ref)
        for c in range(4):
            tasks += head_tasks(256 + c * LANES, qb_ref, c, rope_a,
                                lambda q_h, w: ([(_qk(q_h, kb_ref[ctx, :]), vb_refs[w][ctx, :])], None))
        for c in range(2):
            tasks += head_tasks(768 + c * LANES, qw_ref, c, rope_a,
                                lambda q_h, w, c=c: ([(_qk(q_h, kw_ref[ctx, :]), vw_refs[w][ctx, :])],
                                                     sink_ref[c + 2 * w] * LOG2E))
        for c in range(2):
            def a_score(q_h, w, c=c):
                cs = slice(c * LANES, (c + 1) * LANES)
                return ([(_qk(q_h, ka_ref[ctx, cs]), va_refs[w][ctx, cs])], None)
            tasks += head_tasks(c * LANES, qa_ref, c, lane_lo, a_score)
        run(tasks)

    o = jnp.dot(att_ref[...], wo_ref[...], preferred_element_type=F32)
    o_ref[...] = x_ref[...] + mod_ref[2:3, :] * o


def _ffn_kernel(x_ref, xp_ref, xn_ref, g_ref, mod_ref, wup_ref, cw_ref, cb_ref, wdn_ref, fg_ref,
                o_ref, *, final):
    j = pl.program_id(1)
    xm = x_ref[...]
    xf = jnp.concatenate([xp_ref[...], xm, xn_ref[...]], axis=0)
    h = _norm_mod(xf, g_ref[...], mod_ref[3:4, :], mod_ref[4:5, :])
    ab = jnp.dot(h.astype(BF16), wup_ref[...], preferred_element_type=F32)
    a_mid = ab[HALO:HALO + TM, :D_FF]
    b = ab[HALO:HALO + TM, D_FF:]
    has_prev = jnp.logical_and(j != 0, j != N_LAT)
    has_next = jnp.logical_and(j != N_LAT - 1, j != N_BLK - 1)
    a_prev = jnp.where(has_prev, ab[HALO - 1:HALO, :D_FF], 0.0)
    a_next = jnp.where(has_next, ab[HALO + TM:HALO + TM + 1, :D_FF], 0.0)
    row = lax.broadcasted_iota(jnp.int32, (TM, 1), 0)
    a_up = jnp.where(row == 0, a_prev, pltpu.roll(a_mid, 1, 0))
    a_dn = jnp.where(row == TM - 1, a_next, pltpu.roll(a_mid, TM - 1, 0))
    a = a_up * cw_ref[0:1, :] + a_mid * cw_ref[1:2, :] + a_dn * cw_ref[2:3, :] + cb_ref[...]
    g = (a * jax.nn.sigmoid(a)) * b
    y = jnp.dot(g.astype(BF16), wdn_ref[...], preferred_element_type=F32)
    out = xm + mod_ref[5:6, :] * y
    if final:
        out = (out * lax.rsqrt(jnp.mean(out * out, axis=-1, keepdims=True) + EPS)) * fg_ref[...]
    o_ref[...] = out


def _row_spec(width):
    return pl.BlockSpec((None, TM, width), lambda b, j: (b, j, 0))


def _full_spec(width):
    return pl.BlockSpec((None, ROWS, width), lambda b, j: (b, 0, 0))


def _const_spec(shape):
    nd = len(shape)
    return pl.BlockSpec(shape, lambda b, j: (0,) * nd)


def _layer_spec(l, shape):
    nd = len(shape)
    return pl.BlockSpec((None,) + tuple(shape), lambda b, j: (l,) + (0,) * nd)


def _mod_spec(l):
    return pl.BlockSpec((None, None, N_MOD, D_MODEL),
                        lambda b, j: (l, jnp.where(j < N_LAT, b, BATCH), 0, 0))


def _act_shape(width, dtype=BF16):
    return jax.ShapeDtypeStruct((BATCH, ROWS, width), dtype)


def kernel(x, c, ctx, c_ctx, attn_norm, ffn_norm, w_mod, b_mod, w_in, q_gain, k_gain,
           na_rpb, wa_sink, w_out, w_up, conv_w, conv_b, w_down, final_norm):
    xa = jnp.concatenate([x, ctx], axis=1)
    c_all = jnp.zeros((MOD_ROWS, D_MODEL), F32).at[:BATCH].set(c).at[BATCH].set(c_ctx)
    w_in_p = _take_runs(w_in.astype(BF16), _IN_PERM, 2)
    w_out_p = _take_runs(w_out.astype(BF16), _OUT_PERM, 1)
    w_up_b = w_up.astype(BF16)
    w_dn_b = w_down.astype(BF16)
    qg = q_gain[:, _GAIN_PERM].reshape(DEPTH, 1, LANES)
    kg = k_gain[:, _GAIN_PERM].reshape(DEPTH, 1, LANES)
    cos128, sin128 = _rope_tables()
    an = attn_norm.reshape(DEPTH, 1, D_MODEL)
    fn = ffn_norm.reshape(DEPTH, 1, D_MODEL)
    fg = final_norm.reshape(1, D_MODEL)
    cb = conv_b.reshape(DEPTH, 1, D_FF)
    bm = b_mod.reshape(DEPTH, 1, N_MOD * D_MODEL)

    bias_tab = pl.pallas_call(
        _na_bias_kernel,
        grid=(DEPTH, NA_HEADS),
        in_specs=[pl.BlockSpec((None, None, N_PAIR, LANES), lambda l, h: (l, h, 0, 0))],
        out_specs=pl.BlockSpec((None, None, N_PAIR, GRID_W, LANES), lambda l, h: (l, h, 0, 0, 0)),
        out_shape=jax.ShapeDtypeStruct((DEPTH, NA_HEADS, N_PAIR, GRID_W, LANES), F32),
        compiler_params=_params(2),
        name="nbr_bias",
    )(_na_bias_rows(na_rpb))

    mods = pl.pallas_call(
        _mod_kernel,
        grid=(DEPTH, N_MOD),
        in_specs=[pl.BlockSpec((MOD_ROWS, D_MODEL), lambda l, n: (0, 0)),
                  pl.BlockSpec((None, D_MODEL, D_MODEL), lambda l, n: (l, 0, n)),
                  pl.BlockSpec((None, 1, D_MODEL), lambda l, n: (l, 0, n))],
        out_specs=pl.BlockSpec((None, MOD_ROWS, D_MODEL), lambda l, n: (l, 0, n)),
        out_shape=jax.ShapeDtypeStruct((DEPTH, MOD_ROWS, N_MOD * D_MODEL), F32),
        compiler_params=_params(2),
        name="adaln_mod",
    )(c_all, w_mod, bm)
    mods = mods.reshape(DEPTH, MOD_ROWS, N_MOD, D_MODEL)

    qkv_widths = (512, 256, 256, 128, 128, 128, 128, 128, 128, 256, 256, 256)
    for l in range(DEPTH):
        last = l == DEPTH - 1
        nq = N_LAT if last else N_BLK
        out_rows = SEQ if last else ROWS

        qkv = pl.pallas_call(
            _qkv_kernel,
            grid=(BATCH, N_BLK),
            in_specs=[_row_spec(D_MODEL), _layer_spec(l, (1, D_MODEL)), _mod_spec(l),
                      _layer_spec(l, (D_MODEL, D_PROJ)),
                      pl.BlockSpec((TM, LANES), lambda b, j: (j, 0)),
                      pl.BlockSpec((TM, LANES), lambda b, j: (j, 0)),
                      _layer_spec(l, (1, LANES)), _layer_spec(l, (1, LANES))],
            out_specs=[_row_spec(w) for w in qkv_widths],
            out_shape=[_act_shape(w) for w in qkv_widths],
            compiler_params=_params(2, VMEM_LIMIT),
            name=f"qkv_proj_{l}",
        )(xa, an, mods, w_in_p, cos128, sin128, qg, kg)

        x1 = pl.pallas_call(
            _attn_kernel,
            grid=(BATCH, nq),
            in_specs=[pl.BlockSpec(memory_space=pltpu.SMEM),
                      _row_spec(D_MODEL), _mod_spec(l), _layer_spec(l, (D_MODEL, D_MODEL)),
                      _layer_spec(l, (NA_HEADS, N_PAIR, GRID_W, LANES)),
                      _row_spec(512), _row_spec(256), _row_spec(256)]
                     + [_full_spec(w) for w in qkv_widths[3:]],
            out_specs=_row_spec(D_MODEL),
            out_shape=jax.ShapeDtypeStruct((BATCH, out_rows, D_MODEL), F32),
            scratch_shapes=[pltpu.VMEM((TM, D_MODEL), BF16)],
            compiler_params=_params(2, VMEM_LIMIT),
            name=f"attn_{l}",
        )(wa_sink[l], xa, mods, w_out_p, bias_tab, *qkv)

        blk8 = TM // HALO
        n_halo = out_rows // HALO
        xa = pl.pallas_call(
            functools.partial(_ffn_kernel, final=last),
            grid=(BATCH, nq),
            in_specs=[_row_spec(D_MODEL),
                      pl.BlockSpec((None, HALO, D_MODEL),
                                   lambda b, j: (b, jnp.maximum(j * blk8 - 1, 0), 0)),
                      pl.BlockSpec((None, HALO, D_MODEL),
                                   lambda b, j: (b, jnp.minimum((j + 1) * blk8, n_halo - 1), 0)),
                      _layer_spec(l, (1, D_MODEL)), _mod_spec(l),
                      _layer_spec(l, (D_MODEL, 2 * D_FF)),
                      _layer_spec(l, (CONV_W, D_FF)), _layer_spec(l, (1, D_FF)),
                      _layer_spec(l, (D_FF, D_MODEL)), _const_spec((1, D_MODEL))],
            out_specs=_row_spec(D_MODEL),
            out_shape=jax.ShapeDtypeStruct((BATCH, out_rows, D_MODEL), F32),
            compiler_params=_params(2, VMEM_LIMIT),
            name=f"conv_ffn_{l}",
        )(x1, x1, x1, fn, mods, w_up_b, conv_w, cb, w_dn_b, fg)

    return xa
```

```python
import functools
import math

import numpy as np
import jax
import jax.numpy as jnp
from jax import lax
from jax.experimental import pallas as pl
from jax.experimental.pallas import tpu as pltpu

D_MODEL = 1024
BATCH = 8
SEQ = 2048
DEPTH = 4
GRID_W = 64
CTX_LEN = 256
HEAD_DIM = 64
NA_HEADS = 4
GA_HEADS = 8
GA_KV_HEADS = 2
WA_HEADS = 4
WA_KV_HEADS = 2
NA_WIN_ROWS = 8
NA_WIN_COLS = 16
WA_RADIUS = 128
D_FF = 2816
CONV_W = 3
ROPE_THETA = 10000.0
EPS = 1e-6
N_MOD = 6
D_PROJ = 2048

ROWS = SEQ + CTX_LEN
TM = 256
N_BLK = ROWS // TM
N_LAT = SEQ // TM
GRID_H = SEQ // GRID_W
LANES = 128
HALO = 8
MOD_ROWS = 16
NEG = -1e30
LOG2E = math.log2(math.e)
VMEM_LIMIT = 56 * 1024 * 1024

WA_SPAN = TM + 2 * WA_RADIUS
NA_BLK_ROWS = TM // GRID_W
NA_SPAN_ROWS = 12
NA_SPAN = NA_SPAN_ROWS * GRID_W
N_DR = 2 * NA_WIN_ROWS - 1
N_DC = 2 * NA_WIN_COLS - 1
N_PAIR = 16
OUT_CHUNK = 256
ATTN_AHEAD = 1
QKV_TM = 768
QKV_SUB = 256

F32 = jnp.float32
BF16 = jnp.bfloat16


def _params(n_axes, vmem=None):
    return pltpu.CompilerParams(dimension_semantics=("arbitrary",) * n_axes,
                                vmem_limit_bytes=vmem)


def _rope_lane_perm(base, head_stride, n_chunks):
    cols = []
    for c in range(n_chunks):
        for lane in range(LANES):
            part, hsel, i = lane // 64, (lane // 32) % 2, lane % 32
            cols.append(base + HEAD_DIM * (c + head_stride * hsel) + 32 * part + i)
    return cols


def _in_proj_perm():
    qa, ka, va = 0, 256, 512
    qb, kb, vb = 768, 1280, 1408
    qw, kw, vw = 1536, 1792, 1920
    cols = []
    cols += _rope_lane_perm(qb, 4, 4)
    cols += list(range(qa, qa + 256))
    cols += list(range(ka, ka + 256))
    cols += list(range(va, va + 256))
    cols += _rope_lane_perm(qw, 2, 2)
    cols += _rope_lane_perm(kb, 1, 1)
    cols += list(range(vb, vb + 128))
    cols += _rope_lane_perm(kw, 1, 1)
    cols += list(range(vw, vw + 128))
    return np.asarray(cols, np.int32)


def _out_proj_perm():
    rows = list(range(256))
    for c in range(4):
        for lane in range(LANES):
            rows.append(256 + HEAD_DIM * (c + 4 * (lane // 64)) + lane % 64)
    for c in range(2):
        for lane in range(LANES):
            rows.append(768 + HEAD_DIM * (c + 2 * (lane // 64)) + lane % 64)
    return np.asarray(rows, np.int32)


_IN_PERM = _in_proj_perm()
_OUT_PERM = _out_proj_perm()
_GAIN_PERM = np.asarray([32 * (l // 64) + l % 32 for l in range(LANES)], np.int32)
_PAIR_ROW = np.clip(np.arange(N_PAIR + 1) - 1, 0, N_DR - 1)


def _rope_tables():
    t = jnp.arange(SEQ, dtype=jnp.int32)
    row = (t // GRID_W).astype(F32)
    col = (t % GRID_W).astype(F32)
    n = HEAD_DIM // 4
    inv = ROPE_THETA ** (-jnp.arange(n, dtype=F32) / n)
    ang = jnp.concatenate([row[:, None] * inv, col[:, None] * inv], axis=-1)
    cos, sin = jnp.cos(ang), jnp.sin(ang)
    cos128 = jnp.concatenate([cos, cos, cos, cos], axis=-1)
    sin128 = jnp.concatenate([-sin, -sin, sin, sin], axis=-1)
    cos128 = jnp.concatenate([cos128, jnp.ones((CTX_LEN, LANES), F32)], axis=0)
    sin128 = jnp.concatenate([sin128, jnp.zeros((CTX_LEN, LANES), F32)], axis=0)
    return cos128, sin128


def _na_bias_rows(na_rpb):
    rp = jnp.pad(na_rpb, ((0, 0), (0, 0), (0, 0), (0, GRID_W - N_DC)))
    return jnp.concatenate([rp[:, :, _PAIR_ROW[:-1]], rp[:, :, _PAIR_ROW[1:]]], axis=-1)


def _na_bias_kernel(t_ref, o_ref):
    t = t_ref[...] * LOG2E
    kc = lax.broadcasted_iota(jnp.int32, (N_PAIR, LANES), 1) & (GRID_W - 1)
    for c in range(GRID_W):
        ws = min(max(c - NA_WIN_COLS // 2, 0), GRID_W - NA_WIN_COLS)
        rolled = pltpu.roll(t, (LANES - (NA_WIN_COLS - 1) + c) % LANES, 1)
        o_ref[:, c, :] = jnp.where(jnp.logical_and(kc >= ws, kc < ws + NA_WIN_COLS), rolled, NEG)


def _mod_kernel(c_ref, w_ref, b_ref, o_ref):
    c = c_ref[...]
    act = (c * jax.nn.sigmoid(c)).astype(BF16)
    o_ref[...] = jnp.dot(act, w_ref[...].astype(BF16), preferred_element_type=F32) + b_ref[...]


def _norm_mod(x, gain, shift, scale):
    y = x * lax.rsqrt(jnp.mean(x * x, axis=-1, keepdims=True) + EPS)
    return (y * gain) * (1 + scale) + shift


def _qkv_kernel(x_ref, g_ref, mod_ref, cmod_ref, w_ref, cos_ref, sin_ref, qg_ref, kg_ref,
                qb_ref, qa_ref, qw_ref, kb_ref, vba_ref, vbb_ref, kw_ref, vwa_ref, vwb_ref,
                ka_ref, vaa_ref, vab_ref):
    sub = QKV_SUB
    row0 = pl.program_id(1) * QKV_TM
    lane = lax.broadcasted_iota(jnp.int32, (sub, LANES), 1)
    head_a = ((lane >> 5) & 1) == 0
    lane_lo = lane < HEAD_DIM
    qscale = HEAD_DIM ** -0.5 * LOG2E

    def headnorm(xc, gain):
        sq = xc * xc
        sa = jnp.sum(jnp.where(head_a, sq, 0.0), axis=-1, keepdims=True)
        sb = jnp.sum(jnp.where(head_a, 0.0, sq), axis=-1, keepdims=True)
        ms = jnp.where(head_a, sa, sb) * (1.0 / HEAD_DIM)
        return (xc * lax.rsqrt(ms + EPS)) * gain

    def epilogue(y, rows):
        cosv, sinv = cos_ref[rows, :], sin_ref[rows, :]

        def rope(xc):
            return xc * cosv + pltpu.roll(xc, 64, 1) * sinv

        def chunk(c0):
            return y[:, c0:c0 + LANES]

        def put_v(va_ref, vb_ref, c, vc):
            va_ref[rows, c * LANES:(c + 1) * LANES] = jnp.where(lane_lo, vc, 1.0).astype(BF16)
            vb_ref[rows, c * LANES:(c + 1) * LANES] = jnp.where(lane_lo, 1.0, vc).astype(BF16)

        for c in range(4):
            qb_ref[rows, c * LANES:(c + 1) * LANES] = (
                rope(headnorm(chunk(c * LANES), qg_ref[...])) * qscale).astype(BF16)
        qa_ref[rows, :] = (y[:, 512:768] * qscale).astype(BF16)
        ka_ref[rows, :] = y[:, 768:1024].astype(BF16)
        for c in range(2):
            put_v(vaa_ref, vab_ref, c, chunk(1024 + c * LANES))
            qw_ref[rows, c * LANES:(c + 1) * LANES] = (
                rope(chunk(1280 + c * LANES)) * qscale).astype(BF16)
        kb_ref[rows, :] = rope(headnorm(chunk(1536), kg_ref[...])).astype(BF16)
        put_v(vba_ref, vbb_ref, 0, chunk(1664))
        kw_ref[rows, :] = rope(chunk(1792)).astype(BF16)
        put_v(vwa_ref, vwb_ref, 0, chunk(1920))

    pending = None
    for s in range(QKV_TM // sub):
        rows = slice(s * sub, (s + 1) * sub)
        is_ctx = row0 + s * sub >= SEQ
        shift = jnp.where(is_ctx, cmod_ref[0:1, :], mod_ref[0:1, :])
        scale = jnp.where(is_ctx, cmod_ref[1:2, :], mod_ref[1:2, :])
        h = _norm_mod(x_ref[rows, :], g_ref[...], shift, scale)
        y = jnp.dot(h.astype(BF16), w_ref[...], preferred_element_type=F32)
        if pending is not None:
            epilogue(*pending)
        pending = (y, rows)
    epilogue(*pending)


def _qk(q, k):
    return lax.dot_general(q, k, (((1,), (1,)), ((), ())), preferred_element_type=F32)


def _softmax_pv(parts, extra=None):
    m = None
    for s, _ in parts:
        sm = jnp.max(s, axis=-1, keepdims=True)
        m = sm if m is None else jnp.maximum(m, sm)
    if extra is not None:
        m = jnp.maximum(m, extra)
    o = None
    for s, v in parts:
        pv = jnp.dot(jnp.exp2(s - m).astype(BF16), v, preferred_element_type=F32)
        o = pv if o is None else o + pv
    den = pltpu.roll(o, HEAD_DIM, 1)
    if extra is not None:
        den = den + jnp.exp2(extra - m)
    return o / den


def _attn_kernel(sink_ref, x_ref, mod_ref, wo_ref, bias_ref, qb_ref, qa_ref, qw_ref,
                 kb_ref, vba_ref, vbb_ref, kw_ref, vwa_ref, vwb_ref, ka_ref, vaa_ref, vab_ref,
                 o_ref, att_ref):
    j = pl.program_id(1)
    lane = lax.broadcasted_iota(jnp.int32, (TM, LANES), 1)
    rope_a = ((lane >> 5) & 1) == 0
    lane_lo = lane < HEAD_DIM
    ctx = slice(SEQ, ROWS)

    def heads(q_ref, c, a_mask):
        qf = q_ref[:, c * LANES:(c + 1) * LANES].astype(F32)
        return jnp.where(a_mask, qf, 0.0).astype(BF16), jnp.where(a_mask, 0.0, qf).astype(BF16)

    def run(tasks):
        held = {}
        proj = None
        queue = [task[2]() for task in tasks[:ATTN_AHEAD]]
        for t, (col, which, _) in enumerate(tasks):
            if t + ATTN_AHEAD < len(tasks):
                queue.append(tasks[t + ATTN_AHEAD][2]())
            o = _softmax_pv(*queue.pop(0))
            if which == 0:
                held[col] = o
                continue
            att_ref[:, col:col + LANES] = jnp.where(lane_lo, held.pop(col), o).astype(BF16)
            if col % OUT_CHUNK == OUT_CHUNK - LANES:
                cols = slice(col + LANES - OUT_CHUNK, col + LANES)
                part = jnp.dot(att_ref[:, cols], wo_ref[cols, :], preferred_element_type=F32)
                proj = part if proj is None else proj + part
        o_ref[...] = x_ref[...] + mod_ref[2:3, :] * proj

    def head_tasks(col, q_ref, c, a_mask, score):
        return [(col, w, functools.partial(lambda w: score(heads(q_ref, c, a_mask)[w], w), w))
                for w in range(2)]

    @pl.when(j < N_LAT)
    def _():
        tasks = []
        vb_refs = (vba_ref, vbb_ref)
        for c in range(4):
            tasks += head_tasks(256 + c * LANES, qb_ref, c, rope_a,
                                lambda q_h, w: ([(_qk(q_h, kb_ref[...]), vb_refs[w][...])], None))

        start = pl.multiple_of(jnp.clip(j * TM - WA_RADIUS, 0, SEQ - WA_SPAN), WA_RADIUS)
        loc = pl.ds(start, WA_SPAN)
        qpos = j * TM + lax.broadcasted_iota(jnp.int32, (TM, 1), 0)
        kpos = start + lax.broadcasted_iota(jnp.int32, (1, WA_SPAN), 1)
        near = jnp.abs(qpos - kpos) <= WA_RADIUS
        vw_refs = (vwa_ref, vwb_ref)
        for c in range(2):
            def w_score(q_h, w, c=c):
                return ([(jnp.where(near, _qk(q_h, kw_ref[loc, :]), NEG), vw_refs[w][loc, :]),
                         (_qk(q_h, kw_ref[ctx, :]), vw_refs[w][ctx, :])],
                        sink_ref[c + 2 * w] * LOG2E)
            tasks += head_tasks(768 + c * LANES, qw_ref, c, rope_a, w_score)

        row0 = j * NA_BLK_ROWS
        span0 = jnp.clip(row0 - NA_WIN_ROWS // 2, 0, GRID_H - NA_SPAN_ROWS)
        nloc = pl.ds(pl.multiple_of(span0 * GRID_W, GRID_W), NA_SPAN)
        qrow = row0 + (lax.broadcasted_iota(jnp.int32, (TM, 1), 0) >> 6)
        wrow = jnp.clip(qrow - NA_WIN_ROWS // 2, 0, GRID_H - NA_WIN_ROWS)
        krow = span0 + (lax.broadcasted_iota(jnp.int32, (1, NA_SPAN), 1) >> 6)
        in_rows = jnp.logical_and(krow >= wrow, krow < wrow + NA_WIN_ROWS)
        va_refs = (vaa_ref, vab_ref)
        for c in range(2):
            def a_score(q_h, w, c=c):
                cs = slice(c * LANES, (c + 1) * LANES)
                bias = jnp.concatenate(
                    [jnp.concatenate(
                        [bias_ref[2 * c + w,
                                  jnp.clip(span0 + 2 * p - (row0 + iq) + NA_WIN_ROWS, 0, N_PAIR - 1)]
                         for p in range(NA_SPAN_ROWS // 2)], axis=1)
                     for iq in range(NA_BLK_ROWS)], axis=0)
                s_loc = jnp.where(in_rows, _qk(q_h, ka_ref[nloc, cs]) + bias, NEG)
                return ([(s_loc, va_refs[w][nloc, cs]),
                         (_qk(q_h, ka_ref[ctx, cs]), va_refs[w][ctx, cs])], None)
            tasks += head_tasks(c * LANES, qa_ref, c, lane_lo, a_score)
        run(tasks)

    @pl.when(j >= N_LAT)
    def _():
        tasks = []
        vb_refs, vw_refs, va_refs = (vba_ref, vbb_ref), (vwa_ref, vwb_ref), (vaa_ref, vab_ref)
        for c in range(4):
            tasks += head_tasks(256 + c * LANES, qb_ref, c, rope_a,
                                lambda q_h, w: ([(_qk(q_h, kb_ref[ctx, :]), vb_refs[w][ctx, :])], None))
        for c in range(2):
            tasks += head_tasks(768 + c * LANES, qw_ref, c, rope_a,
                                lambda q_h, w, c=c: ([(_qk(q_h, kw_ref[ctx, :]), vw_refs[w][ctx, :])],
                                                     sink_ref[c + 2 * w] * LOG2E))
        for c in range(2):
            def a_score(q_h, w, c=c):
                cs = slice(c * LANES, (c + 1) * LANES)
                return ([(_qk(q_h, ka_ref[ctx, cs]), va_refs[w][ctx, cs])], None)
            tasks += head_tasks(c * LANES, qa_ref, c, lane_lo, a_score)
        run(tasks)


def _ffn_kernel(x_ref, xp_ref, xn_ref, g_ref, mod_ref, wup_ref, cw_ref, cb_ref, wdn_ref, fg_ref,
                o_ref, *, final):
    j = pl.program_id(1)
    xm = x_ref[...]
    xf = jnp.concatenate([xp_ref[...], xm, xn_ref[...]], axis=0)
    h = _norm_mod(xf, g_ref[...], mod_ref[3:4, :], mod_ref[4:5, :])
    af = jnp.dot(h.astype(BF16), wup_ref[:, :D_FF], preferred_element_type=F32)
    b = jnp.dot(h[HALO:HALO + TM].astype(BF16), wup_ref[:, D_FF:], preferred_element_type=F32)
    a_mid = af[HALO:HALO + TM]
    has_prev = jnp.logical_and(j != 0, j != N_LAT)
    has_next = jnp.logical_and(j != N_LAT - 1, j != N_BLK - 1)
    a_prev = jnp.where(has_prev, af[HALO - 1:HALO], 0.0)
    a_next = jnp.where(has_next, af[HALO + TM:HALO + TM + 1], 0.0)
    row = lax.broadcasted_iota(jnp.int32, (TM, 1), 0)
    a_up = jnp.where(row == 0, a_prev, pltpu.roll(a_mid, 1, 0))
    a_dn = jnp.where(row == TM - 1, a_next, pltpu.roll(a_mid, TM - 1, 0))
    a = a_up * cw_ref[0:1, :] + a_mid * cw_ref[1:2, :] + a_dn * cw_ref[2:3, :] + cb_ref[...]
    g = (a * jax.nn.sigmoid(a)) * b
    y = jnp.dot(g.astype(BF16), wdn_ref[...], preferred_element_type=F32)
    out = xm + mod_ref[5:6, :] * y
    if final:
        out = (out * lax.rsqrt(jnp.mean(out * out, axis=-1, keepdims=True) + EPS)) * fg_ref[...]
    o_ref[...] = out


def _row_spec(width, rows=TM):
    return pl.BlockSpec((None, rows, width), lambda b, j: (b, j, 0))


def _full_spec(width):
    return pl.BlockSpec((None, ROWS, width), lambda b, j: (b, 0, 0))


def _const_spec(shape):
    nd = len(shape)
    return pl.BlockSpec(shape, lambda b, j: (0,) * nd)


def _layer_spec(l, shape):
    nd = len(shape)
    return pl.BlockSpec((None,) + tuple(shape), lambda b, j: (l,) + (0,) * nd)


def _mod_spec(l):
    return pl.BlockSpec((None, None, N_MOD, D_MODEL),
                        lambda b, j: (l, jnp.where(j < N_LAT, b, BATCH), 0, 0))


def _act_shape(width, dtype=BF16):
    return jax.ShapeDtypeStruct((BATCH, ROWS, width), dtype)


def kernel(x, c, ctx, c_ctx, attn_norm, ffn_norm, w_mod, b_mod, w_in, q_gain, k_gain,
           na_rpb, wa_sink, w_out, w_up, conv_w, conv_b, w_down, final_norm):
    xa = jnp.concatenate([x, ctx], axis=1)
    c_all = jnp.zeros((MOD_ROWS, D_MODEL), F32).at[:BATCH].set(c).at[BATCH].set(c_ctx)
    w_in_p = w_in[:, :, _IN_PERM].astype(BF16)
    w_out_p = w_out[:, _OUT_PERM, :].astype(BF16)
    w_up_b = w_up.astype(BF16)
    w_dn_b = w_down.astype(BF16)
    qg = q_gain[:, _GAIN_PERM].reshape(DEPTH, 1, LANES)
    kg = k_gain[:, _GAIN_PERM].reshape(DEPTH, 1, LANES)
    cos128, sin128 = _rope_tables()
    an = attn_norm.reshape(DEPTH, 1, D_MODEL)
    fn = ffn_norm.reshape(DEPTH, 1, D_MODEL)
    fg = final_norm.reshape(1, D_MODEL)
    cb = conv_b.reshape(DEPTH, 1, D_FF)
    bm = b_mod.reshape(DEPTH, 1, N_MOD * D_MODEL)

    bias_tab = pl.pallas_call(
        _na_bias_kernel,
        grid=(DEPTH, NA_HEADS),
        in_specs=[pl.BlockSpec((None, None, N_PAIR, LANES), lambda l, h: (l, h, 0, 0))],
        out_specs=pl.BlockSpec((None, None, N_PAIR, GRID_W, LANES), lambda l, h: (l, h, 0, 0, 0)),
        out_shape=jax.ShapeDtypeStruct((DEPTH, NA_HEADS, N_PAIR, GRID_W, LANES), F32),
        compiler_params=_params(2),
        name="nbr_bias",
    )(_na_bias_rows(na_rpb))

    mods = pl.pallas_call(
        _mod_kernel,
        grid=(DEPTH, N_MOD),
        in_specs=[pl.BlockSpec((MOD_ROWS, D_MODEL), lambda l, n: (0, 0)),
                  pl.BlockSpec((None, D_MODEL, D_MODEL), lambda l, n: (l, 0, n)),
                  pl.BlockSpec((None, 1, D_MODEL), lambda l, n: (l, 0, n))],
        out_specs=pl.BlockSpec((None, MOD_ROWS, D_MODEL), lambda l, n: (l, 0, n)),
        out_shape=jax.ShapeDtypeStruct((DEPTH, MOD_ROWS, N_MOD * D_MODEL), F32),
        compiler_params=_params(2),
        name="adaln_mod",
    )(c_all, w_mod, bm)
    mods = mods.reshape(DEPTH, MOD_ROWS, N_MOD, D_MODEL)

    qkv_widths = (512, 256, 256, 128, 128, 128, 128, 128, 128, 256, 256, 256)
    for l in range(DEPTH):
        last = l == DEPTH - 1
        nq = N_LAT if last else N_BLK
        out_rows = SEQ if last else ROWS

        qkv = pl.pallas_call(
            _qkv_kernel,
            grid=(BATCH, ROWS // QKV_TM),
            in_specs=[_row_spec(D_MODEL, QKV_TM), _layer_spec(l, (1, D_MODEL)),
                      pl.BlockSpec((None, None, N_MOD, D_MODEL), lambda b, j: (l, b, 0, 0)),
                      pl.BlockSpec((None, None, N_MOD, D_MODEL), lambda b, j: (l, BATCH, 0, 0)),
                      _layer_spec(l, (D_MODEL, D_PROJ)),
                      pl.BlockSpec((QKV_TM, LANES), lambda b, j: (j, 0)),
                      pl.BlockSpec((QKV_TM, LANES), lambda b, j: (j, 0)),
                      _layer_spec(l, (1, LANES)), _layer_spec(l, (1, LANES))],
            out_specs=[_row_spec(w, QKV_TM) for w in qkv_widths],
            out_shape=[_act_shape(w) for w in qkv_widths],
            compiler_params=_params(2, VMEM_LIMIT),
            name=f"qkv_proj_{l}",
        )(xa, an, mods, mods, w_in_p, cos128, sin128, qg, kg)

        x1 = pl.pallas_call(
            _attn_kernel,
            grid=(BATCH, nq),
            in_specs=[pl.BlockSpec(memory_space=pltpu.SMEM),
                      _row_spec(D_MODEL), _mod_spec(l), _layer_spec(l, (D_MODEL, D_MODEL)),
                      _layer_spec(l, (NA_HEADS, N_PAIR, GRID_W, LANES)),
                      _row_spec(512), _row_spec(256), _row_spec(256)]
                     + [_full_spec(w) for w in qkv_widths[3:]],
            out_specs=_row_spec(D_MODEL),
            out_shape=jax.ShapeDtypeStruct((BATCH, out_rows, D_MODEL), F32),
            scratch_shapes=[pltpu.VMEM((TM, D_MODEL), BF16)],
            compiler_params=_params(2, VMEM_LIMIT),
            name=f"attn_{l}",
        )(wa_sink[l], xa, mods, w_out_p, bias_tab, *qkv)

        blk8 = TM // HALO
        n_halo = out_rows // HALO
        xa = pl.pallas_call(
            functools.partial(_ffn_kernel, final=last),
            grid=(BATCH, nq),
            in_specs=[_row_spec(D_MODEL),
                      pl.BlockSpec((None, HALO, D_MODEL),
                                   lambda b, j: (b, jnp.maximum(j * blk8 - 1, 0), 0)),
                      pl.BlockSpec((None, HALO, D_MODEL),
                                   lambda b, j: (b, jnp.minimum((j + 1) * blk8, n_halo - 1), 0)),
                      _layer_spec(l, (1, D_MODEL)), _mod_spec(l),
                      _layer_spec(l, (D_MODEL, 2 * D_FF)),
                      _layer_spec(l, (CONV_W, D_FF)), _layer_spec(l, (1, D_FF)),
                      _layer_spec(l, (D_FF, D_MODEL)), _const_spec((1, D_MODEL))],
            out_specs=_row_spec(D_MODEL),
            out_shape=jax.ShapeDtypeStruct((BATCH, out_rows, D_MODEL), F32),
            compiler_params=_params(2, VMEM_LIMIT),
            name=f"conv_ffn_{l}",
        )(x1, x1, x1, fn, mods, w_up_b, conv_w, cb, w_dn_b, fg)

    return xa
```

```python
import functools
import math

import numpy as np
import jax
import jax.numpy as jnp
from jax import lax
from jax.experimental import pallas as pl
from jax.experimental.pallas import tpu as pltpu

D_MODEL = 1024
BATCH = 8
SEQ = 2048
DEPTH = 4
GRID_W = 64
CTX_LEN = 256
HEAD_DIM = 64
NA_HEADS = 4
GA_HEADS = 8
GA_KV_HEADS = 2
WA_HEADS = 4
WA_KV_HEADS = 2
NA_WIN_ROWS = 8
NA_WIN_COLS = 16
WA_RADIUS = 128
D_FF = 2816
CONV_W = 3
ROPE_THETA = 10000.0
EPS = 1e-6
N_MOD = 6
D_PROJ = 2048

ROWS = SEQ + CTX_LEN
TM = 256
N_BLK = ROWS // TM
N_LAT = SEQ // TM
GRID_H = SEQ // GRID_W
LANES = 128
HALO = 8
MOD_ROWS = 16
NEG = -1e30
LOG2E = math.log2(math.e)
VMEM_LIMIT = 56 * 1024 * 1024

WA_SPAN = TM + 2 * WA_RADIUS
NA_BLK_ROWS = TM // GRID_W
NA_SPAN_ROWS = 12
NA_SPAN = NA_SPAN_ROWS * GRID_W
N_DR = 2 * NA_WIN_ROWS - 1
N_DC = 2 * NA_WIN_COLS - 1
N_PAIR = 16
ATT_TM = 512
ATTN_AHEAD = 1
QKV_TM = 768
QKV_SUB = 256

F32 = jnp.float32
BF16 = jnp.bfloat16


def _params(n_axes, vmem=None):
    return pltpu.CompilerParams(dimension_semantics=("arbitrary",) * n_axes,
                                vmem_limit_bytes=vmem)


def _rope_lane_perm(base, head_stride, n_chunks):
    cols = []
    for c in range(n_chunks):
        for lane in range(LANES):
            part, hsel, i = lane // 64, (lane // 32) % 2, lane % 32
            cols.append(base + HEAD_DIM * (c + head_stride * hsel) + 32 * part + i)
    return cols


def _in_proj_perm():
    qa, ka, va = 0, 256, 512
    qb, kb, vb = 768, 1280, 1408
    qw, kw, vw = 1536, 1792, 1920
    cols = []
    cols += _rope_lane_perm(qb, 4, 4)
    cols += list(range(qa, qa + 256))
    cols += list(range(ka, ka + 256))
    cols += list(range(va, va + 256))
    cols += _rope_lane_perm(qw, 2, 2)
    cols += _rope_lane_perm(kb, 1, 1)
    cols += list(range(vb, vb + 128))
    cols += _rope_lane_perm(kw, 1, 1)
    cols += list(range(vw, vw + 128))
    return np.asarray(cols, np.int32)


def _out_proj_perm():
    rows = list(range(256))
    for c in range(4):
        for lane in range(LANES):
            rows.append(256 + HEAD_DIM * (c + 4 * (lane // 64)) + lane % 64)
    for c in range(2):
        for lane in range(LANES):
            rows.append(768 + HEAD_DIM * (c + 2 * (lane // 64)) + lane % 64)
    return np.asarray(rows, np.int32)


_IN_PERM = _in_proj_perm()
_OUT_PERM = _out_proj_perm()
_GAIN_PERM = np.asarray([32 * (l // 64) + l % 32 for l in range(LANES)], np.int32)
_PAIR_ROW = np.clip(np.arange(N_PAIR + 1) - 1, 0, N_DR - 1)


def _take_runs(a, perm, axis):
    cuts = [0] + [i for i in range(1, len(perm)) if perm[i] != perm[i - 1] + 1] + [len(perm)]
    return jnp.concatenate(
        [lax.slice_in_dim(a, int(perm[s]), int(perm[e - 1]) + 1, axis=axis)
         for s, e in zip(cuts[:-1], cuts[1:])], axis=axis)


def _rope_tables():
    t = jnp.arange(SEQ, dtype=jnp.int32)
    row = (t // GRID_W).astype(F32)
    col = (t % GRID_W).astype(F32)
    n = HEAD_DIM // 4
    inv = ROPE_THETA ** (-jnp.arange(n, dtype=F32) / n)
    ang = jnp.concatenate([row[:, None] * inv, col[:, None] * inv], axis=-1)
    cos, sin = jnp.cos(ang), jnp.sin(ang)
    cos128 = jnp.concatenate([cos, cos, cos, cos], axis=-1)
    sin128 = jnp.concatenate([-sin, -sin, sin, sin], axis=-1)
    cos128 = jnp.concatenate([cos128, jnp.ones((CTX_LEN, LANES), F32)], axis=0)
    sin128 = jnp.concatenate([sin128, jnp.zeros((CTX_LEN, LANES), F32)], axis=0)
    return cos128, sin128


def _na_bias_rows(na_rpb):
    rp = jnp.pad(na_rpb, ((0, 0), (0, 0), (0, 0), (0, GRID_W - N_DC)))
    return jnp.concatenate([rp[:, :, _PAIR_ROW[:-1]], rp[:, :, _PAIR_ROW[1:]]], axis=-1)


def _na_bias_kernel(t_ref, o_ref):
    t = t_ref[...] * LOG2E
    kc = lax.broadcasted_iota(jnp.int32, (N_PAIR, LANES), 1) & (GRID_W - 1)
    for c in range(GRID_W):
        ws = min(max(c - NA_WIN_COLS // 2, 0), GRID_W - NA_WIN_COLS)
        rolled = pltpu.roll(t, (LANES - (NA_WIN_COLS - 1) + c) % LANES, 1)
        o_ref[:, c, :] = jnp.where(jnp.logical_and(kc >= ws, kc < ws + NA_WIN_COLS), rolled, NEG)


def _mod_kernel(c_ref, w_ref, b_ref, o_ref):
    c = c_ref[...]
    act = (c * jax.nn.sigmoid(c)).astype(BF16)
    o_ref[...] = jnp.dot(act, w_ref[...].astype(BF16), preferred_element_type=F32) + b_ref[...]


def _norm_mod(x, gain, shift, scale):
    y = x * lax.rsqrt(jnp.mean(x * x, axis=-1, keepdims=True) + EPS)
    return (y * gain) * (1 + scale) + shift


def _qkv_kernel(x_ref, g_ref, mod_ref, cmod_ref, w_ref, cos_ref, sin_ref, qg_ref, kg_ref,
                qb_ref, qa_ref, qw_ref, kb_ref, vba_ref, vbb_ref, kw_ref, vwa_ref, vwb_ref,
                ka_ref, vaa_ref, vab_ref):
    sub = QKV_SUB
    row0 = pl.program_id(1) * QKV_TM
    lane = lax.broadcasted_iota(jnp.int32, (sub, LANES), 1)
    head_a = ((lane >> 5) & 1) == 0
    lane_lo = lane < HEAD_DIM
    qscale = HEAD_DIM ** -0.5 * LOG2E

    def headnorm(xc, gain):
        sq = xc * xc
        sa = jnp.sum(jnp.where(head_a, sq, 0.0), axis=-1, keepdims=True)
        sb = jnp.sum(jnp.where(head_a, 0.0, sq), axis=-1, keepdims=True)
        ms = jnp.where(head_a, sa, sb) * (1.0 / HEAD_DIM)
        return (xc * lax.rsqrt(ms + EPS)) * gain

    def epilogue(y, rows):
        cosv, sinv = cos_ref[rows, :], sin_ref[rows, :]

        def rope(xc):
            return xc * cosv + pltpu.roll(xc, 64, 1) * sinv

        def chunk(c0):
            return y[:, c0:c0 + LANES]

        def put_v(va_ref, vb_ref, c, vc):
            va_ref[rows, c * LANES:(c + 1) * LANES] = jnp.where(lane_lo, vc, 1.0).astype(BF16)
            vb_ref[rows, c * LANES:(c + 1) * LANES] = jnp.where(lane_lo, 1.0, vc).astype(BF16)

        for c in range(4):
            qb_ref[rows, c * LANES:(c + 1) * LANES] = (
                rope(headnorm(chunk(c * LANES), qg_ref[...])) * qscale).astype(BF16)
        qa_ref[rows, :] = (y[:, 512:768] * qscale).astype(BF16)
        ka_ref[rows, :] = y[:, 768:1024].astype(BF16)
        for c in range(2):
            put_v(vaa_ref, vab_ref, c, chunk(1024 + c * LANES))
            qw_ref[rows, c * LANES:(c + 1) * LANES] = (
                rope(chunk(1280 + c * LANES)) * qscale).astype(BF16)
        kb_ref[rows, :] = rope(headnorm(chunk(1536), kg_ref[...])).astype(BF16)
        put_v(vba_ref, vbb_ref, 0, chunk(1664))
        kw_ref[rows, :] = rope(chunk(1792)).astype(BF16)
        put_v(vwa_ref, vwb_ref, 0, chunk(1920))

    pending = None
    for s in range(QKV_TM // sub):
        rows = slice(s * sub, (s + 1) * sub)
        is_ctx = row0 + s * sub >= SEQ
        shift = jnp.where(is_ctx, cmod_ref[0:1, :], mod_ref[0:1, :])
        scale = jnp.where(is_ctx, cmod_ref[1:2, :], mod_ref[1:2, :])
        h = _norm_mod(x_ref[rows, :], g_ref[...], shift, scale)
        y = jnp.dot(h.astype(BF16), w_ref[...], preferred_element_type=F32)
        if pending is not None:
            epilogue(*pending)
        pending = (y, rows)
    epilogue(*pending)


def _qk(q, k):
    return lax.dot_general(q, k, (((1,), (1,)), ((), ())), preferred_element_type=F32)


def _softmax_pv(parts, extra=None):
    m = None
    for s, _ in parts:
        sm = jnp.max(s, axis=-1, keepdims=True)
        m = sm if m is None else jnp.maximum(m, sm)
    if extra is not None:
        m = jnp.maximum(m, extra)
    o = None
    for s, v in parts:
        pv = jnp.dot(jnp.exp2(s - m).astype(BF16), v, preferred_element_type=F32)
        o = pv if o is None else o + pv
    den = pltpu.roll(o, HEAD_DIM, 1)
    if extra is not None:
        den = den + jnp.exp2(extra - m)
    return o / den


def _split_heads(q, a_mask):
    qf = q.astype(F32)
    return jnp.where(a_mask, qf, 0.0).astype(BF16), jnp.where(a_mask, 0.0, qf).astype(BF16)


def _head_tasks(rows, col, q_ref, c, a_mask, score):
    def task(w):
        return score(_split_heads(q_ref[rows, c * LANES:(c + 1) * LANES], a_mask)[w], w)
    return [(rows, col, w, functools.partial(task, w)) for w in range(2)]


def _run_heads(tasks, att_ref):
    lane_lo = lax.broadcasted_iota(jnp.int32, (TM, LANES), 1) < HEAD_DIM
    held = {}
    queue = [task[3]() for task in tasks[:ATTN_AHEAD]]
    for t, (rows, col, which, _) in enumerate(tasks):
        if t + ATTN_AHEAD < len(tasks):
            queue.append(tasks[t + ATTN_AHEAD][3]())
        o = _softmax_pv(*queue.pop(0))
        if which == 0:
            held[(rows.start, col)] = o
        else:
            att_ref[rows, col:col + LANES] = jnp.where(
                lane_lo, held.pop((rows.start, col)), o).astype(BF16)


def _attn_kernel(sink_ref, x_ref, mod_ref, wo_ref, bias_ref, qb_ref, qa_ref, qw_ref,
                 kb_ref, vba_ref, vbb_ref, kw_ref, vwa_ref, vwb_ref, ka_ref, vaa_ref, vab_ref,
                 o_ref, att_ref):
    lane = lax.broadcasted_iota(jnp.int32, (TM, LANES), 1)
    rope_a = ((lane >> 5) & 1) == 0
    lane_lo = lane < HEAD_DIM
    ctx = slice(SEQ, ROWS)
    vb_refs, vw_refs, va_refs = (vba_ref, vbb_ref), (vwa_ref, vwb_ref), (vaa_ref, vab_ref)

    tasks = []
    for sb in range(ATT_TM // TM):
        j = pl.program_id(1) * (ATT_TM // TM) + sb
        rows = slice(sb * TM, (sb + 1) * TM)

        for c in range(4):
            tasks += _head_tasks(rows, 256 + c * LANES, qb_ref, c, rope_a,
                                 lambda q_h, w: ([(_qk(q_h, kb_ref[...]), vb_refs[w][...])], None))

        start = pl.multiple_of(jnp.clip(j * TM - WA_RADIUS, 0, SEQ - WA_SPAN), WA_RADIUS)
        loc = pl.ds(start, WA_SPAN)
        qpos = j * TM + lax.broadcasted_iota(jnp.int32, (TM, 1), 0)
        kpos = start + lax.broadcasted_iota(jnp.int32, (1, WA_SPAN), 1)
        near = jnp.abs(qpos - kpos) <= WA_RADIUS
        for c in range(2):
            def w_score(q_h, w, c=c, loc=loc, near=near):
                return ([(jnp.where(near, _qk(q_h, kw_ref[loc, :]), NEG), vw_refs[w][loc, :]),
                         (_qk(q_h, kw_ref[ctx, :]), vw_refs[w][ctx, :])],
                        sink_ref[c + 2 * w] * LOG2E)
            tasks += _head_tasks(rows, 768 + c * LANES, qw_ref, c, rope_a, w_score)

        row0 = j * NA_BLK_ROWS
        span0 = jnp.clip(row0 - NA_WIN_ROWS // 2, 0, GRID_H - NA_SPAN_ROWS)
        nloc = pl.ds(pl.multiple_of(span0 * GRID_W, GRID_W), NA_SPAN)
        qrow = row0 + (lax.broadcasted_iota(jnp.int32, (TM, 1), 0) >> 6)
        wrow = jnp.clip(qrow - NA_WIN_ROWS // 2, 0, GRID_H - NA_WIN_ROWS)
        krow = span0 + (lax.broadcasted_iota(jnp.int32, (1, NA_SPAN), 1) >> 6)
        in_rows = jnp.logical_and(krow >= wrow, krow < wrow + NA_WIN_ROWS)
        for c in range(2):
            def a_score(q_h, w, c=c, row0=row0, span0=span0, nloc=nloc, in_rows=in_rows):
                cs = slice(c * LANES, (c + 1) * LANES)
                bias = jnp.concatenate(
                    [jnp.concatenate(
                        [bias_ref[2 * c + w,
                                  jnp.clip(span0 + 2 * p - (row0 + iq) + NA_WIN_ROWS, 0, N_PAIR - 1)]
                         for p in range(NA_SPAN_ROWS // 2)], axis=1)
                     for iq in range(NA_BLK_ROWS)], axis=0)
                s_loc = jnp.where(in_rows, _qk(q_h, ka_ref[nloc, cs]) + bias, NEG)
                return ([(s_loc, va_refs[w][nloc, cs]),
                         (_qk(q_h, ka_ref[ctx, cs]), va_refs[w][ctx, cs])], None)
            tasks += _head_tasks(rows, c * LANES, qa_ref, c, lane_lo, a_score)

    _run_heads(tasks, att_ref)
    o = jnp.dot(att_ref[...], wo_ref[...], preferred_element_type=F32)
    o_ref[...] = x_ref[...] + mod_ref[2:3, :] * o


def _ctx_attn_kernel(sink_ref, x_ref, mod_ref, wo_ref, qb_ref, qa_ref, qw_ref,
                     kb_ref, vba_ref, vbb_ref, kw_ref, vwa_ref, vwb_ref, ka_ref, vaa_ref, vab_ref,
                     x1_ref, o_ref, att_ref):
    del x1_ref
    lane = lax.broadcasted_iota(jnp.int32, (TM, LANES), 1)
    rope_a = ((lane >> 5) & 1) == 0
    lane_lo = lane < HEAD_DIM
    rows = slice(0, TM)
    vb_refs, vw_refs, va_refs = (vba_ref, vbb_ref), (vwa_ref, vwb_ref), (vaa_ref, vab_ref)
    tasks = []
    for c in range(4):
        tasks += _head_tasks(rows, 256 + c * LANES, qb_ref, c, rope_a,
                             lambda q_h, w: ([(_qk(q_h, kb_ref[...]), vb_refs[w][...])], None))
    for c in range(2):
        tasks += _head_tasks(rows, 768 + c * LANES, qw_ref, c, rope_a,
                             lambda q_h, w, c=c: ([(_qk(q_h, kw_ref[...]), vw_refs[w][...])],
                                                  sink_ref[c + 2 * w] * LOG2E))
    for c in range(2):
        def a_score(q_h, w, c=c):
            cs = slice(c * LANES, (c + 1) * LANES)
            return ([(_qk(q_h, ka_ref[:, cs]), va_refs[w][:, cs])], None)
        tasks += _head_tasks(rows, c * LANES, qa_ref, c, lane_lo, a_score)
    _run_heads(tasks, att_ref)
    o = jnp.dot(att_ref[...], wo_ref[...], preferred_element_type=F32)
    o_ref[...] = x_ref[...] + mod_ref[2:3, :] * o


def _ffn_kernel(x_ref, xp_ref, xn_ref, g_ref, mod_ref, wup_ref, cw_ref, cb_ref, wdn_ref, fg_ref,
                o_ref, *, final):
    j = pl.program_id(1)
    xm = x_ref[...]
    xf = jnp.concatenate([xp_ref[...], xm, xn_ref[...]], axis=0)
    h = _norm_mod(xf, g_ref[...], mod_ref[3:4, :], mod_ref[4:5, :])
    af = jnp.dot(h.astype(BF16), wup_ref[:, :D_FF], preferred_element_type=F32)
    b = jnp.dot(h[HALO:HALO + TM].astype(BF16), wup_ref[:, D_FF:], preferred_element_type=F32)
    a_mid = af[HALO:HALO + TM]
    has_prev = jnp.logical_and(j != 0, j != N_LAT)
    has_next = jnp.logical_and(j != N_LAT - 1, j != N_BLK - 1)
    a_prev = jnp.where(has_prev, af[HALO - 1:HALO], 0.0)
    a_next = jnp.where(has_next, af[HALO + TM:HALO + TM + 1], 0.0)
    row = lax.broadcasted_iota(jnp.int32, (TM, 1), 0)
    a_up = jnp.where(row == 0, a_prev, pltpu.roll(a_mid, 1, 0))
    a_dn = jnp.where(row == TM - 1, a_next, pltpu.roll(a_mid, TM - 1, 0))
    a = a_up * cw_ref[0:1, :] + a_mid * cw_ref[1:2, :] + a_dn * cw_ref[2:3, :] + cb_ref[...]
    g = (a * jax.nn.sigmoid(a)) * b
    y = jnp.dot(g.astype(BF16), wdn_ref[...], preferred_element_type=F32)
    out = xm + mod_ref[5:6, :] * y
    if final:
        out = (out * lax.rsqrt(jnp.mean(out * out, axis=-1, keepdims=True) + EPS)) * fg_ref[...]
    o_ref[...] = out


def _row_spec(width, rows=TM):
    return pl.BlockSpec((None, rows, width), lambda b, j: (b, j, 0))


def _full_spec(width):
    return pl.BlockSpec((None, ROWS, width), lambda b, j: (b, 0, 0))


def _const_spec(shape):
    nd = len(shape)
    return pl.BlockSpec(shape, lambda b, j: (0,) * nd)


def _layer_spec(l, shape):
    nd = len(shape)
    return pl.BlockSpec((None,) + tuple(shape), lambda b, j: (l,) + (0,) * nd)


def _mod_spec(l):
    return pl.BlockSpec((None, None, N_MOD, D_MODEL),
                        lambda b, j: (l, jnp.where(j < N_LAT, b, BATCH), 0, 0))


def _act_shape(width, dtype=BF16):
    return jax.ShapeDtypeStruct((BATCH, ROWS, width), dtype)


def kernel(x, c, ctx, c_ctx, attn_norm, ffn_norm, w_mod, b_mod, w_in, q_gain, k_gain,
           na_rpb, wa_sink, w_out, w_up, conv_w, conv_b, w_down, final_norm):
    xa = jnp.concatenate([x, ctx], axis=1)
    c_all = jnp.zeros((MOD_ROWS, D_MODEL), F32).at[:BATCH].set(c).at[BATCH].set(c_ctx)
    w_in_p = _take_runs(w_in.astype(BF16), _IN_PERM, 2)
    w_out_p = _take_runs(w_out.astype(BF16), _OUT_PERM, 1)
    w_up_b = w_up.astype(BF16)
    w_dn_b = w_down.astype(BF16)
    qg = q_gain[:, _GAIN_PERM].reshape(DEPTH, 1, LANES)
    kg = k_gain[:, _GAIN_PERM].reshape(DEPTH, 1, LANES)
    cos128, sin128 = _rope_tables()
    an = attn_norm.reshape(DEPTH, 1, D_MODEL)
    fn = ffn_norm.reshape(DEPTH, 1, D_MODEL)
    fg = final_norm.reshape(1, D_MODEL)
    cb = conv_b.reshape(DEPTH, 1, D_FF)
    bm = b_mod.reshape(DEPTH, 1, N_MOD * D_MODEL)

    bias_tab = pl.pallas_call(
        _na_bias_kernel,
        grid=(DEPTH, NA_HEADS),
        in_specs=[pl.BlockSpec((None, None, N_PAIR, LANES), lambda l, h: (l, h, 0, 0))],
        out_specs=pl.BlockSpec((None, None, N_PAIR, GRID_W, LANES), lambda l, h: (l, h, 0, 0, 0)),
        out_shape=jax.ShapeDtypeStruct((DEPTH, NA_HEADS, N_PAIR, GRID_W, LANES), F32),
        compiler_params=_params(2),
        name="nbr_bias",
    )(_na_bias_rows(na_rpb))

    mods = pl.pallas_call(
        _mod_kernel,
        grid=(DEPTH, N_MOD),
        in_specs=[pl.BlockSpec((MOD_ROWS, D_MODEL), lambda l, n: (0, 0)),
                  pl.BlockSpec((None, D_MODEL, D_MODEL), lambda l, n: (l, 0, n)),
                  pl.BlockSpec((None, 1, D_MODEL), lambda l, n: (l, 0, n))],
        out_specs=pl.BlockSpec((None, MOD_ROWS, D_MODEL), lambda l, n: (l, 0, n)),
        out_shape=jax.ShapeDtypeStruct((DEPTH, MOD_ROWS, N_MOD * D_MODEL), F32),
        compiler_params=_params(2),
        name="adaln_mod",
    )(c_all, w_mod, bm)
    mods = mods.reshape(DEPTH, MOD_ROWS, N_MOD, D_MODEL)

    qkv_widths = (512, 256, 256, 128, 128, 128, 128, 128, 128, 256, 256, 256)
    for l in range(DEPTH):
        last = l == DEPTH - 1
        nq = N_LAT if last else N_BLK
        out_rows = SEQ if last else ROWS

        qkv = pl.pallas_call(
            _qkv_kernel,
            grid=(BATCH, ROWS // QKV_TM),
            in_specs=[_row_spec(D_MODEL, QKV_TM), _layer_spec(l, (1, D_MODEL)),
                      pl.BlockSpec((None, None, N_MOD, D_MODEL), lambda b, j: (l, b, 0, 0)),
                      pl.BlockSpec((None, None, N_MOD, D_MODEL), lambda b, j: (l, BATCH, 0, 0)),
                      _layer_spec(l, (D_MODEL, D_PROJ)),
                      pl.BlockSpec((QKV_TM, LANES), lambda b, j: (j, 0)),
                      pl.BlockSpec((QKV_TM, LANES), lambda b, j: (j, 0)),
                      _layer_spec(l, (1, LANES)), _layer_spec(l, (1, LANES))],
            out_specs=[_row_spec(w, QKV_TM) for w in qkv_widths],
            out_shape=[_act_shape(w) for w in qkv_widths],
            compiler_params=_params(2, VMEM_LIMIT),
            name=f"qkv_proj_{l}",
        )(xa, an, mods, mods, w_in_p, cos128, sin128, qg, kg)

        x1 = pl.pallas_call(
            _attn_kernel,
            grid=(BATCH, SEQ // ATT_TM),
            in_specs=[pl.BlockSpec(memory_space=pltpu.SMEM),
                      _row_spec(D_MODEL, ATT_TM), _mod_spec(l), _layer_spec(l, (D_MODEL, D_MODEL)),
                      _layer_spec(l, (NA_HEADS, N_PAIR, GRID_W, LANES)),
                      _row_spec(512, ATT_TM), _row_spec(256, ATT_TM), _row_spec(256, ATT_TM)]
                     + [_full_spec(w) for w in qkv_widths[3:]],
            out_specs=_row_spec(D_MODEL, ATT_TM),
            out_shape=jax.ShapeDtypeStruct((BATCH, out_rows, D_MODEL), F32),
            scratch_shapes=[pltpu.VMEM((ATT_TM, D_MODEL), BF16)],
            compiler_params=_params(2, VMEM_LIMIT),
            name=f"attn_{l}",
        )(wa_sink[l], xa, mods, w_out_p, bias_tab, *qkv)

        if not last:
            ctx_spec = lambda w: pl.BlockSpec((None, TM, w), lambda b: (b, N_LAT, 0))
            x1 = pl.pallas_call(
                _ctx_attn_kernel,
                grid=(BATCH,),
                in_specs=[pl.BlockSpec(memory_space=pltpu.SMEM), ctx_spec(D_MODEL),
                          pl.BlockSpec((None, None, N_MOD, D_MODEL), lambda b: (l, BATCH, 0, 0)),
                          pl.BlockSpec((None, D_MODEL, D_MODEL), lambda b: (l, 0, 0))]
                         + [ctx_spec(w) for w in qkv_widths]
                         + [pl.BlockSpec(memory_space=pl.ANY)],
                out_specs=ctx_spec(D_MODEL),
                out_shape=jax.ShapeDtypeStruct((BATCH, ROWS, D_MODEL), F32),
                scratch_shapes=[pltpu.VMEM((TM, D_MODEL), BF16)],
                input_output_aliases={4 + len(qkv_widths): 0},
                compiler_params=_params(1, VMEM_LIMIT),
                name=f"ctx_attn_{l}",
            )(wa_sink[l], xa, mods, w_out_p, *qkv, x1)

        blk8 = TM // HALO
        n_halo = out_rows // HALO
        xa = pl.pallas_call(
            functools.partial(_ffn_kernel, final=last),
            grid=(BATCH, nq),
            in_specs=[_row_spec(D_MODEL),
                      pl.BlockSpec((None, HALO, D_MODEL),
                                   lambda b, j: (b, jnp.maximum(j * blk8 - 1, 0), 0)),
                      pl.BlockSpec((None, HALO, D_MODEL),
                                   lambda b, j: (b, jnp.minimum((j + 1) * blk8, n_halo - 1), 0)),
                      _layer_spec(l, (1, D_MODEL)), _mod_spec(l),
                      _layer_spec(l, (D_MODEL, 2 * D_FF)),
                      _layer_spec(l, (CONV_W, D_FF)), _layer_spec(l, (1, D_FF)),
                      _layer_spec(l, (D_FF, D_MODEL)), _const_spec((1, D_MODEL))],
            out_specs=_row_spec(D_MODEL),
            out_shape=jax.ShapeDtypeStruct((BATCH, out_rows, D_MODEL), F32),
            compiler_params=_params(2, VMEM_LIMIT),
            name=f"conv_ffn_{l}",
        )(x1, x1, x1, fn, mods, w_up_b, conv_w, cb, w_dn_b, fg)

    return xa
```

```python
import functools
import math

import numpy as np
import jax
import jax.numpy as jnp
from jax import lax
from jax.experimental import pallas as pl
from jax.experimental.pallas import tpu as pltpu

D_MODEL = 1024
BATCH = 8
SEQ = 2048
DEPTH = 4
GRID_W = 64
CTX_LEN = 256
HEAD_DIM = 64
NA_HEADS = 4
GA_HEADS = 8
GA_KV_HEADS = 2
WA_HEADS = 4
WA_KV_HEADS = 2
NA_WIN_ROWS = 8
NA_WIN_COLS = 16
WA_RADIUS = 128
D_FF = 2816
CONV_W = 3
ROPE_THETA = 10000.0
EPS = 1e-6
N_MOD = 6
D_PROJ = 2048

ROWS = SEQ + CTX_LEN
TM = 256
N_BLK = ROWS // TM
N_LAT = SEQ // TM
GRID_H = SEQ // GRID_W
LANES = 128
HALO = 8
MOD_ROWS = 16
NEG = -1e30
LOG2E = math.log2(math.e)
VMEM_LIMIT = 56 * 1024 * 1024

WA_SPAN = TM + 2 * WA_RADIUS
NA_BLK_ROWS = TM // GRID_W
NA_SPAN_ROWS = 12
NA_SPAN = NA_SPAN_ROWS * GRID_W
N_DR = 2 * NA_WIN_ROWS - 1
N_DC = 2 * NA_WIN_COLS - 1
N_PAIR = 16
ATT_TM = 512
ATTN_AHEAD = 1
QKV_TM = 768
QKV_SUB = 256

F32 = jnp.float32
BF16 = jnp.bfloat16


def _params(n_axes, vmem=None):
    return pltpu.CompilerParams(dimension_semantics=("arbitrary",) * n_axes,
                                vmem_limit_bytes=vmem)


def _rope_lane_perm(base, head_stride, n_chunks):
    cols = []
    for c in range(n_chunks):
        for lane in range(LANES):
            part, hsel, i = lane // 64, (lane // 32) % 2, lane % 32
            cols.append(base + HEAD_DIM * (c + head_stride * hsel) + 32 * part + i)
    return cols


def _in_proj_perm():
    qa, ka, va = 0, 256, 512
    qb, kb, vb = 768, 1280, 1408
    qw, kw, vw = 1536, 1792, 1920
    cols = []
    cols += _rope_lane_perm(qb, 4, 4)
    cols += list(range(qa, qa + 256))
    cols += list(range(ka, ka + 256))
    cols += list(range(va, va + 256))
    cols += _rope_lane_perm(qw, 2, 2)
    cols += _rope_lane_perm(kb, 1, 1)
    cols += list(range(vb, vb + 128))
    cols += _rope_lane_perm(kw, 1, 1)
    cols += list(range(vw, vw + 128))
    return np.asarray(cols, np.int32)


def _out_proj_perm():
    rows = list(range(256))
    for c in range(4):
        for lane in range(LANES):
            rows.append(256 + HEAD_DIM * (c + 4 * (lane // 64)) + lane % 64)
    for c in range(2):
        for lane in range(LANES):
            rows.append(768 + HEAD_DIM * (c + 2 * (lane // 64)) + lane % 64)
    return np.asarray(rows, np.int32)


_IN_PERM = _in_proj_perm()
_OUT_PERM = _out_proj_perm()
_GAIN_PERM = np.asarray([32 * (l // 64) + l % 32 for l in range(LANES)], np.int32)
_PAIR_ROW = np.clip(np.arange(N_PAIR + 1) - 1, 0, N_DR - 1)


def _take_runs(a, perm, axis):
    cuts = [0] + [i for i in range(1, len(perm)) if perm[i] != perm[i - 1] + 1] + [len(perm)]
    return jnp.concatenate(
        [lax.slice_in_dim(a, int(perm[s]), int(perm[e - 1]) + 1, axis=axis)
         for s, e in zip(cuts[:-1], cuts[1:])], axis=axis)


def _rope_tables():
    t = jnp.arange(SEQ, dtype=jnp.int32)
    row = (t // GRID_W).astype(F32)
    col = (t % GRID_W).astype(F32)
    n = HEAD_DIM // 4
    inv = ROPE_THETA ** (-jnp.arange(n, dtype=F32) / n)
    ang = jnp.concatenate([row[:, None] * inv, col[:, None] * inv], axis=-1)
    cos, sin = jnp.cos(ang), jnp.sin(ang)
    cos128 = jnp.concatenate([cos, cos, cos, cos], axis=-1)
    sin128 = jnp.concatenate([-sin, -sin, sin, sin], axis=-1)
    cos128 = jnp.concatenate([cos128, jnp.ones((CTX_LEN, LANES), F32)], axis=0)
    sin128 = jnp.concatenate([sin128, jnp.zeros((CTX_LEN, LANES), F32)], axis=0)
    return cos128, sin128


def _na_bias_rows(na_rpb):
    rp = jnp.pad(na_rpb, ((0, 0), (0, 0), (0, 0), (0, GRID_W - N_DC)))
    return jnp.concatenate([rp[:, :, _PAIR_ROW[:-1]], rp[:, :, _PAIR_ROW[1:]]], axis=-1)


def _na_bias_kernel(t_ref, o_ref):
    t = t_ref[...] * LOG2E
    kc = lax.broadcasted_iota(jnp.int32, (N_PAIR, LANES), 1) & (GRID_W - 1)
    for c in range(GRID_W):
        ws = min(max(c - NA_WIN_COLS // 2, 0), GRID_W - NA_WIN_COLS)
        rolled = pltpu.roll(t, (LANES - (NA_WIN_COLS - 1) + c) % LANES, 1)
        o_ref[:, c, :] = jnp.where(jnp.logical_and(kc >= ws, kc < ws + NA_WIN_COLS), rolled, NEG)


def _mod_kernel(c_ref, w_ref, b_ref, o_ref):
    c = c_ref[...]
    act = (c * jax.nn.sigmoid(c)).astype(BF16)
    o_ref[...] = jnp.dot(act, w_ref[...].astype(BF16), preferred_element_type=F32) + b_ref[...]


def _norm_mod(x, gain, shift, scale):
    y = x * lax.rsqrt(jnp.mean(x * x, axis=-1, keepdims=True) + EPS)
    return (y * gain) * (1 + scale) + shift


def _qkv_kernel(x_ref, g_ref, mod_ref, cmod_ref, w_ref, cos_ref, sin_ref, qg_ref, kg_ref,
                qb_ref, qa_ref, qw_ref, kb_ref, vba_ref, vbb_ref, kw_ref, vwa_ref, vwb_ref,
                ka_ref, vaa_ref, vab_ref):
    sub = QKV_SUB
    row0 = pl.program_id(1) * QKV_TM
    lane = lax.broadcasted_iota(jnp.int32, (sub, LANES), 1)
    head_a = ((lane >> 5) & 1) == 0
    lane_lo = lane < HEAD_DIM
    qscale = HEAD_DIM ** -0.5 * LOG2E

    def headnorm(xc, gain):
        sq = xc * xc
        sa = jnp.sum(jnp.where(head_a, sq, 0.0), axis=-1, keepdims=True)
        sb = jnp.sum(jnp.where(head_a, 0.0, sq), axis=-1, keepdims=True)
        ms = jnp.where(head_a, sa, sb) * (1.0 / HEAD_DIM)
        return (xc * lax.rsqrt(ms + EPS)) * gain

    def epilogue(y, rows):
        cosv, sinv = cos_ref[rows, :], sin_ref[rows, :]

        def rope(xc):
            return xc * cosv + pltpu.roll(xc, 64, 1) * sinv

        def chunk(c0):
            return y[:, c0:c0 + LANES]

        def put_v(va_ref, vb_ref, c, vc):
            va_ref[rows, c * LANES:(c + 1) * LANES] = jnp.where(lane_lo, vc, 1.0).astype(BF16)
            vb_ref[rows, c * LANES:(c + 1) * LANES] = jnp.where(lane_lo, 1.0, vc).astype(BF16)

        for c in range(4):
            qb_ref[rows, c * LANES:(c + 1) * LANES] = (
                rope(headnorm(chunk(c * LANES), qg_ref[...])) * qscale).astype(BF16)
        qa_ref[rows, :] = (y[:, 512:768] * qscale).astype(BF16)
        ka_ref[rows, :] = y[:, 768:1024].astype(BF16)
        for c in range(2):
            put_v(vaa_ref, vab_ref, c, chunk(1024 + c * LANES))
            qw_ref[rows, c * LANES:(c + 1) * LANES] = (
                rope(chunk(1280 + c * LANES)) * qscale).astype(BF16)
        kb_ref[rows, :] = rope(headnorm(chunk(1536), kg_ref[...])).astype(BF16)
        put_v(vba_ref, vbb_ref, 0, chunk(1664))
        kw_ref[rows, :] = rope(chunk(1792)).astype(BF16)
        put_v(vwa_ref, vwb_ref, 0, chunk(1920))

    pending = None
    for s in range(QKV_TM // sub):
        rows = slice(s * sub, (s + 1) * sub)
        is_ctx = row0 + s * sub >= SEQ
        shift = jnp.where(is_ctx, cmod_ref[0:1, :], mod_ref[0:1, :])
        scale = jnp.where(is_ctx, cmod_ref[1:2, :], mod_ref[1:2, :])
        h = _norm_mod(x_ref[rows, :], g_ref[...], shift, scale)
        y = jnp.dot(h.astype(BF16), w_ref[...], preferred_element_type=F32)
        if pending is not None:
            epilogue(*pending)
        pending = (y, rows)
    epilogue(*pending)


def _qk(q, k):
    return lax.dot_general(q, k, (((1,), (1,)), ((), ())), preferred_element_type=F32)


def _softmax_pv(parts, extra=None):
    m = None
    for s, _ in parts:
        sm = jnp.max(s, axis=-1, keepdims=True)
        m = sm if m is None else jnp.maximum(m, sm)
    if extra is not None:
        m = jnp.maximum(m, extra)
    o = None
    for s, v in parts:
        pv = jnp.dot(jnp.exp2(s - m).astype(BF16), v, preferred_element_type=F32)
        o = pv if o is None else o + pv
    den = pltpu.roll(o, HEAD_DIM, 1)
    if extra is not None:
        den = den + jnp.exp2(extra - m)
    return o / den


def _split_heads(q, a_mask):
    qf = q.astype(F32)
    return jnp.where(a_mask, qf, 0.0).astype(BF16), jnp.where(a_mask, 0.0, qf).astype(BF16)


def _head_tasks(rows, col, q_ref, c, a_mask, score):
    def task(w):
        return score(_split_heads(q_ref[rows, c * LANES:(c + 1) * LANES], a_mask)[w], w)
    return [(rows, col, w, functools.partial(task, w)) for w in range(2)]


def _run_heads(tasks, att_ref):
    lane_lo = lax.broadcasted_iota(jnp.int32, (TM, LANES), 1) < HEAD_DIM
    held = {}
    queue = [task[3]() for task in tasks[:ATTN_AHEAD]]
    for t, (rows, col, which, _) in enumerate(tasks):
        if t + ATTN_AHEAD < len(tasks):
            queue.append(tasks[t + ATTN_AHEAD][3]())
        o = _softmax_pv(*queue.pop(0))
        if which == 0:
            held[(rows.start, col)] = o
        else:
            att_ref[rows, col:col + LANES] = jnp.where(
                lane_lo, held.pop((rows.start, col)), o).astype(BF16)


def _attn_kernel(sink_ref, x_ref, mod_ref, wo_ref, bias_ref, qb_ref, qa_ref, qw_ref,
                 kb_ref, vba_ref, vbb_ref, kw_ref, vwa_ref, vwb_ref, ka_ref, vaa_ref, vab_ref,
                 o_ref, att_ref):
    lane = lax.broadcasted_iota(jnp.int32, (TM, LANES), 1)
    rope_a = ((lane >> 5) & 1) == 0
    lane_lo = lane < HEAD_DIM
    ctx = slice(SEQ, ROWS)
    vb_refs, vw_refs, va_refs = (vba_ref, vbb_ref), (vwa_ref, vwb_ref), (vaa_ref, vab_ref)

    tasks = []
    for sb in range(ATT_TM // TM):
        j = pl.program_id(1) * (ATT_TM // TM) + sb
        rows = slice(sb * TM, (sb + 1) * TM)

        for c in range(4):
            tasks += _head_tasks(rows, 256 + c * LANES, qb_ref, c, rope_a,
                                 lambda q_h, w: ([(_qk(q_h, kb_ref[...]), vb_refs[w][...])], None))

        start = pl.multiple_of(jnp.clip(j * TM - WA_RADIUS, 0, SEQ - WA_SPAN), WA_RADIUS)
        loc = pl.ds(start, WA_SPAN)
        qpos = j * TM + lax.broadcasted_iota(jnp.int32, (TM, 1), 0)
        kpos = start + lax.broadcasted_iota(jnp.int32, (1, WA_SPAN), 1)
        near = jnp.abs(qpos - kpos) <= WA_RADIUS
        for c in range(2):
            def w_score(q_h, w, c=c, loc=loc, near=near):
                return ([(jnp.where(near, _qk(q_h, kw_ref[loc, :]), NEG), vw_refs[w][loc, :]),
                         (_qk(q_h, kw_ref[ctx, :]), vw_refs[w][ctx, :])],
                        sink_ref[c + 2 * w] * LOG2E)
            tasks += _head_tasks(rows, 768 + c * LANES, qw_ref, c, rope_a, w_score)

        row0 = j * NA_BLK_ROWS
        span0 = jnp.clip(row0 - NA_WIN_ROWS // 2, 0, GRID_H - NA_SPAN_ROWS)
        nloc = pl.ds(pl.multiple_of(span0 * GRID_W, GRID_W), NA_SPAN)
        qrow = row0 + (lax.broadcasted_iota(jnp.int32, (TM, 1), 0) >> 6)
        wrow = jnp.clip(qrow - NA_WIN_ROWS // 2, 0, GRID_H - NA_WIN_ROWS)
        krow = span0 + (lax.broadcasted_iota(jnp.int32, (1, NA_SPAN), 1) >> 6)
        in_rows = jnp.logical_and(krow >= wrow, krow < wrow + NA_WIN_ROWS)
        for c in range(2):
            def a_score(q_h, w, c=c, row0=row0, span0=span0, nloc=nloc, in_rows=in_rows):
                cs = slice(c * LANES, (c + 1) * LANES)
                bias = jnp.concatenate(
                    [jnp.concatenate(
                        [bias_ref[2 * c + w,
                                  jnp.clip(span0 + 2 * p - (row0 + iq) + NA_WIN_ROWS, 0, N_PAIR - 1)]
                         for p in range(NA_SPAN_ROWS // 2)], axis=1)
                     for iq in range(NA_BLK_ROWS)], axis=0)
                s_loc = jnp.where(in_rows, _qk(q_h, ka_ref[nloc, cs]) + bias, NEG)
                return ([(s_loc, va_refs[w][nloc, cs]),
                         (_qk(q_h, ka_ref[ctx, cs]), va_refs[w][ctx, cs])], None)
            tasks += _head_tasks(rows, c * LANES, qa_ref, c, lane_lo, a_score)

    _run_heads(tasks, att_ref)
    o = jnp.dot(att_ref[...], wo_ref[...], preferred_element_type=F32)
    o_ref[...] = x_ref[...] + mod_ref[2:3, :] * o


def _ctx_attn_kernel(sink_ref, x_ref, mod_ref, wo_ref, qb_ref, qa_ref, qw_ref,
                     kb_ref, vba_ref, vbb_ref, kw_ref, vwa_ref, vwb_ref, ka_ref, vaa_ref, vab_ref,
                     o_ref, att_ref):
    lane = lax.broadcasted_iota(jnp.int32, (TM, LANES), 1)
    rope_a = ((lane >> 5) & 1) == 0
    lane_lo = lane < HEAD_DIM

    def attend(q_ref, chunks, a_mask, k, v_a, v_b, col, extra=None):
        halves = [_split_heads(q_ref[:, c * LANES:(c + 1) * LANES], a_mask) for c in chunks]
        n = len(chunks) * TM
        s = _qk(jnp.concatenate([h[0] for h in halves] + [h[1] for h in halves], axis=0), k)
        m = jnp.max(s, axis=-1, keepdims=True)
        if extra is not None:
            m = jnp.maximum(m, extra)
        p = jnp.exp2(s - m).astype(BF16)
        o = jnp.concatenate([jnp.dot(p[:n], v_a, preferred_element_type=F32),
                             jnp.dot(p[n:], v_b, preferred_element_type=F32)], axis=0)
        den = pltpu.roll(o, HEAD_DIM, 1)
        if extra is not None:
            den = den + jnp.exp2(extra - m)
        o = o / den
        for i in range(len(chunks)):
            att_ref[:, col + i * LANES:col + (i + 1) * LANES] = jnp.where(
                lane_lo, o[i * TM:(i + 1) * TM], o[n + i * TM:n + (i + 1) * TM]).astype(BF16)

    attend(qb_ref, range(4), rope_a, kb_ref[...], vba_ref[...], vbb_ref[...], 256)
    sinks = jnp.concatenate([jnp.full((TM, 1), sink_ref[h] * LOG2E, F32) for h in range(WA_HEADS)], axis=0)
    attend(qw_ref, range(2), rope_a, kw_ref[...], vwa_ref[...], vwb_ref[...], 768, extra=sinks)
    for c in range(2):
        cs = slice(c * LANES, (c + 1) * LANES)
        attend(qa_ref, [c], lane_lo, ka_ref[:, cs], vaa_ref[:, cs], vab_ref[:, cs], c * LANES)
    o = jnp.dot(att_ref[...], wo_ref[...], preferred_element_type=F32)
    o_ref[...] = x_ref[...] + mod_ref[2:3, :] * o


def _ffn_kernel(x_ref, xc_ref, xp_ref, xn_ref, g_ref, mod_ref, wup_ref, cw_ref, cb_ref, wdn_ref,
                fg_ref, o_ref, *, final):
    j = pl.program_id(1)
    xm = x_ref[...] if final else jnp.where(j < N_LAT, x_ref[...], xc_ref[...])
    xf = jnp.concatenate([xp_ref[...], xm, xn_ref[...]], axis=0)
    h = _norm_mod(xf, g_ref[...], mod_ref[3:4, :], mod_ref[4:5, :])
    af = jnp.dot(h.astype(BF16), wup_ref[:, :D_FF], preferred_element_type=F32)
    b = jnp.dot(h[HALO:HALO + TM].astype(BF16), wup_ref[:, D_FF:], preferred_element_type=F32)
    a_mid = af[HALO:HALO + TM]
    has_prev = jnp.logical_and(j != 0, j != N_LAT)
    has_next = jnp.logical_and(j != N_LAT - 1, j != N_BLK - 1)
    a_prev = jnp.where(has_prev, af[HALO - 1:HALO], 0.0)
    a_next = jnp.where(has_next, af[HALO + TM:HALO + TM + 1], 0.0)
    row = lax.broadcasted_iota(jnp.int32, (TM, 1), 0)
    a_up = jnp.where(row == 0, a_prev, pltpu.roll(a_mid, 1, 0))
    a_dn = jnp.where(row == TM - 1, a_next, pltpu.roll(a_mid, TM - 1, 0))
    a = a_up * cw_ref[0:1, :] + a_mid * cw_ref[1:2, :] + a_dn * cw_ref[2:3, :] + cb_ref[...]
    g = (a * jax.nn.sigmoid(a)) * b
    y = jnp.dot(g.astype(BF16), wdn_ref[...], preferred_element_type=F32)
    out = xm + mod_ref[5:6, :] * y
    if final:
        out = (out * lax.rsqrt(jnp.mean(out * out, axis=-1, keepdims=True) + EPS)) * fg_ref[...]
    o_ref[...] = out


def _row_spec(width, rows=TM):
    return pl.BlockSpec((None, rows, width), lambda b, j: (b, j, 0))


def _full_spec(width):
    return pl.BlockSpec((None, ROWS, width), lambda b, j: (b, 0, 0))


def _const_spec(shape):
    nd = len(shape)
    return pl.BlockSpec(shape, lambda b, j: (0,) * nd)


def _layer_spec(l, shape):
    nd = len(shape)
    return pl.BlockSpec((None,) + tuple(shape), lambda b, j: (l,) + (0,) * nd)


def _mod_spec(l):
    return pl.BlockSpec((None, None, N_MOD, D_MODEL),
                        lambda b, j: (l, jnp.where(j < N_LAT, b, BATCH), 0, 0))


def _act_shape(width, dtype=BF16):
    return jax.ShapeDtypeStruct((BATCH, ROWS, width), dtype)


def kernel(x, c, ctx, c_ctx, attn_norm, ffn_norm, w_mod, b_mod, w_in, q_gain, k_gain,
           na_rpb, wa_sink, w_out, w_up, conv_w, conv_b, w_down, final_norm):
    xa = jnp.concatenate([x, ctx], axis=1)
    c_all = jnp.zeros((MOD_ROWS, D_MODEL), F32).at[:BATCH].set(c).at[BATCH].set(c_ctx)
    w_in_p = _take_runs(w_in.astype(BF16), _IN_PERM, 2)
    w_out_p = _take_runs(w_out.astype(BF16), _OUT_PERM, 1)
    w_up_b = w_up.astype(BF16)
    w_dn_b = w_down.astype(BF16)
    qg = q_gain[:, _GAIN_PERM].reshape(DEPTH, 1, LANES)
    kg = k_gain[:, _GAIN_PERM].reshape(DEPTH, 1, LANES)
    cos128, sin128 = _rope_tables()
    an = attn_norm.reshape(DEPTH, 1, D_MODEL)
    fn = ffn_norm.reshape(DEPTH, 1, D_MODEL)
    fg = final_norm.reshape(1, D_MODEL)
    cb = conv_b.reshape(DEPTH, 1, D_FF)
    bm = b_mod.reshape(DEPTH, 1, N_MOD * D_MODEL)

    bias_tab = pl.pallas_call(
        _na_bias_kernel,
        grid=(DEPTH, NA_HEADS),
        in_specs=[pl.BlockSpec((None, None, N_PAIR, LANES), lambda l, h: (l, h, 0, 0))],
        out_specs=pl.BlockSpec((None, None, N_PAIR, GRID_W, LANES), lambda l, h: (l, h, 0, 0, 0)),
        out_shape=jax.ShapeDtypeStruct((DEPTH, NA_HEADS, N_PAIR, GRID_W, LANES), F32),
        compiler_params=_params(2),
        name="nbr_bias",
    )(_na_bias_rows(na_rpb))

    mods = pl.pallas_call(
        _mod_kernel,
        grid=(DEPTH, N_MOD),
        in_specs=[pl.BlockSpec((MOD_ROWS, D_MODEL), lambda l, n: (0, 0)),
                  pl.BlockSpec((None, D_MODEL, D_MODEL), lambda l, n: (l, 0, n)),
                  pl.BlockSpec((None, 1, D_MODEL), lambda l, n: (l, 0, n))],
        out_specs=pl.BlockSpec((None, MOD_ROWS, D_MODEL), lambda l, n: (l, 0, n)),
        out_shape=jax.ShapeDtypeStruct((DEPTH, MOD_ROWS, N_MOD * D_MODEL), F32),
        compiler_params=_params(2),
        name="adaln_mod",
    )(c_all, w_mod, bm)
    mods = mods.reshape(DEPTH, MOD_ROWS, N_MOD, D_MODEL)

    qkv_widths = (512, 256, 256, 128, 128, 128, 128, 128, 128, 256, 256, 256)
    for l in range(DEPTH):
        last = l == DEPTH - 1
        nq = N_LAT if last else N_BLK
        out_rows = SEQ if last else ROWS

        qkv = pl.pallas_call(
            _qkv_kernel,
            grid=(BATCH, ROWS // QKV_TM),
            in_specs=[_row_spec(D_MODEL, QKV_TM), _layer_spec(l, (1, D_MODEL)),
                      pl.BlockSpec((None, None, N_MOD, D_MODEL), lambda b, j: (l, b, 0, 0)),
                      pl.BlockSpec((None, None, N_MOD, D_MODEL), lambda b, j: (l, BATCH, 0, 0)),
                      _layer_spec(l, (D_MODEL, D_PROJ)),
                      pl.BlockSpec((QKV_TM, LANES), lambda b, j: (j, 0)),
                      pl.BlockSpec((QKV_TM, LANES), lambda b, j: (j, 0)),
                      _layer_spec(l, (1, LANES)), _layer_spec(l, (1, LANES))],
            out_specs=[_row_spec(w, QKV_TM) for w in qkv_widths],
            out_shape=[_act_shape(w) for w in qkv_widths],
            compiler_params=_params(2, VMEM_LIMIT),
            name=f"qkv_proj_{l}",
        )(xa, an, mods, mods, w_in_p, cos128, sin128, qg, kg)

        x1 = pl.pallas_call(
            _attn_kernel,
            grid=(BATCH, SEQ // ATT_TM),
            in_specs=[pl.BlockSpec(memory_space=pltpu.SMEM),
                      _row_spec(D_MODEL, ATT_TM), _mod_spec(l), _layer_spec(l, (D_MODEL, D_MODEL)),
                      _layer_spec(l, (NA_HEADS, N_PAIR, GRID_W, LANES)),
                      _row_spec(512, ATT_TM), _row_spec(256, ATT_TM), _row_spec(256, ATT_TM)]
                     + [_full_spec(w) for w in qkv_widths[3:]],
            out_specs=_row_spec(D_MODEL, ATT_TM),
            out_shape=jax.ShapeDtypeStruct((BATCH, SEQ, D_MODEL), F32),
            scratch_shapes=[pltpu.VMEM((ATT_TM, D_MODEL), BF16)],
            compiler_params=_params(2, VMEM_LIMIT),
            name=f"attn_{l}",
        )(wa_sink[l], xa, mods, w_out_p, bias_tab, *qkv)

        if last:
            x1c = x1
        else:
            ctx_spec = lambda w: pl.BlockSpec((None, TM, w), lambda b: (b, N_LAT, 0))
            x1c = pl.pallas_call(
                _ctx_attn_kernel,
                grid=(BATCH,),
                in_specs=[pl.BlockSpec(memory_space=pltpu.SMEM), ctx_spec(D_MODEL),
                          pl.BlockSpec((None, None, N_MOD, D_MODEL), lambda b: (l, BATCH, 0, 0)),
                          pl.BlockSpec((None, D_MODEL, D_MODEL), lambda b: (l, 0, 0))]
                         + [ctx_spec(w) for w in qkv_widths],
                out_specs=pl.BlockSpec((None, CTX_LEN, D_MODEL), lambda b: (b, 0, 0)),
                out_shape=jax.ShapeDtypeStruct((BATCH, CTX_LEN, D_MODEL), F32),
                scratch_shapes=[pltpu.VMEM((TM, D_MODEL), BF16)],
                compiler_params=_params(1, VMEM_LIMIT),
                name=f"ctx_attn_{l}",
            )(wa_sink[l], xa, mods, w_out_p, *qkv)

        blk8 = TM // HALO
        n_halo = SEQ // HALO
        xa = pl.pallas_call(
            functools.partial(_ffn_kernel, final=last),
            grid=(BATCH, nq),
            in_specs=[pl.BlockSpec((None, TM, D_MODEL),
                                   lambda b, j: (b, jnp.minimum(j, N_LAT - 1), 0)),
                      pl.BlockSpec((None, CTX_LEN, D_MODEL), lambda b, j: (b, 0, 0)),
                      pl.BlockSpec((None, HALO, D_MODEL),
                                   lambda b, j: (b, jnp.maximum(j * blk8 - 1, 0), 0)),
                      pl.BlockSpec((None, HALO, D_MODEL),
                                   lambda b, j: (b, jnp.minimum((j + 1) * blk8, n_halo - 1), 0)),
                      _layer_spec(l, (1, D_MODEL)), _mod_spec(l),
                      _layer_spec(l, (D_MODEL, 2 * D_FF)),
                      _layer_spec(l, (CONV_W, D_FF)), _layer_spec(l, (1, D_FF)),
                      _layer_spec(l, (D_FF, D_MODEL)), _const_spec((1, D_MODEL))],
            out_specs=_row_spec(D_MODEL),
            out_shape=jax.ShapeDtypeStruct((BATCH, out_rows, D_MODEL), F32),
            compiler_params=_params(2, VMEM_LIMIT),
            name=f"conv_ffn_{l}",
        )(x1, x1c, x1, x1, fn, mods, w_up_b, conv_w, cb, w_dn_b, fg)

    return xa
```

```python
import functools
import math

import numpy as np
import jax
import jax.numpy as jnp
from jax import lax
from jax.experimental import pallas as pl
from jax.experimental.pallas import tpu as pltpu

D_MODEL = 1024
BATCH = 8
SEQ = 2048
DEPTH = 4
GRID_W = 64
CTX_LEN = 256
HEAD_DIM = 64
NA_HEADS = 4
GA_HEADS = 8
GA_KV_HEADS = 2
WA_HEADS = 4
WA_KV_HEADS = 2
NA_WIN_ROWS = 8
NA_WIN_COLS = 16
WA_RADIUS = 128
D_FF = 2816
CONV_W = 3
ROPE_THETA = 10000.0
EPS = 1e-6
N_MOD = 6
D_PROJ = 2048

ROWS = SEQ + CTX_LEN
TM = 256
N_BLK = ROWS // TM
N_LAT = SEQ // TM
GRID_H = SEQ // GRID_W
LANES = 128
HALO = 8
MOD_ROWS = 16
NEG = -1e30
LOG2E = math.log2(math.e)
VMEM_LIMIT = 56 * 1024 * 1024

WA_SPAN = TM + 2 * WA_RADIUS
NA_BLK_ROWS = TM // GRID_W
NA_SPAN_ROWS = 12
NA_SPAN = NA_SPAN_ROWS * GRID_W
N_DR = 2 * NA_WIN_ROWS - 1
N_DC = 2 * NA_WIN_COLS - 1
N_PAIR = 16
ATT_TM = 512
ATTN_AHEAD = 1
QKV_TM = 2304
QKV_SUB = 256

F32 = jnp.float32
BF16 = jnp.bfloat16


def _params(n_axes, vmem=None):
    return pltpu.CompilerParams(dimension_semantics=("arbitrary",) * n_axes,
                                vmem_limit_bytes=vmem)


def _rope_lane_perm(base, head_stride, n_chunks):
    cols = []
    for c in range(n_chunks):
        for lane in range(LANES):
            part, hsel, i = lane // 64, (lane // 32) % 2, lane % 32
            cols.append(base + HEAD_DIM * (c + head_stride * hsel) + 32 * part + i)
    return cols


def _in_proj_perm():
    qa, ka, va = 0, 256, 512
    qb, kb, vb = 768, 1280, 1408
    qw, kw, vw = 1536, 1792, 1920
    cols = []
    cols += _rope_lane_perm(qb, 4, 4)
    cols += list(range(qa, qa + 256))
    cols += list(range(ka, ka + 256))
    cols += list(range(va, va + 256))
    cols += _rope_lane_perm(qw, 2, 2)
    cols += _rope_lane_perm(kb, 1, 1)
    cols += list(range(vb, vb + 128))
    cols += _rope_lane_perm(kw, 1, 1)
    cols += list(range(vw, vw + 128))
    return np.asarray(cols, np.int32)


def _out_proj_perm():
    rows = list(range(256))
    for c in range(4):
        for lane in range(LANES):
            rows.append(256 + HEAD_DIM * (c + 4 * (lane // 64)) + lane % 64)
    for c in range(2):
        for lane in range(LANES):
            rows.append(768 + HEAD_DIM * (c + 2 * (lane // 64)) + lane % 64)
    return np.asarray(rows, np.int32)


_IN_PERM = _in_proj_perm()
_OUT_PERM = _out_proj_perm()
_GAIN_PERM = np.asarray([32 * (l // 64) + l % 32 for l in range(LANES)], np.int32)
_PAIR_ROW = np.clip(np.arange(N_PAIR + 1) - 1, 0, N_DR - 1)


def _take_runs(a, perm, axis):
    cuts = [0] + [i for i in range(1, len(perm)) if perm[i] != perm[i - 1] + 1] + [len(perm)]
    return jnp.concatenate(
        [lax.slice_in_dim(a, int(perm[s]), int(perm[e - 1]) + 1, axis=axis)
         for s, e in zip(cuts[:-1], cuts[1:])], axis=axis)


def _rope_tables():
    t = jnp.arange(SEQ, dtype=jnp.int32)
    row = (t // GRID_W).astype(F32)
    col = (t % GRID_W).astype(F32)
    n = HEAD_DIM // 4
    inv = ROPE_THETA ** (-jnp.arange(n, dtype=F32) / n)
    ang = jnp.concatenate([row[:, None] * inv, col[:, None] * inv], axis=-1)
    cos, sin = jnp.cos(ang), jnp.sin(ang)
    cos128 = jnp.concatenate([cos, cos, cos, cos], axis=-1)
    sin128 = jnp.concatenate([-sin, -sin, sin, sin], axis=-1)
    cos128 = jnp.concatenate([cos128, jnp.ones((CTX_LEN, LANES), F32)], axis=0)
    sin128 = jnp.concatenate([sin128, jnp.zeros((CTX_LEN, LANES), F32)], axis=0)
    return cos128, sin128


def _na_bias_rows(na_rpb):
    rp = jnp.pad(na_rpb, ((0, 0), (0, 0), (0, 0), (0, GRID_W - N_DC)))
    return jnp.concatenate([rp[:, :, _PAIR_ROW[:-1]], rp[:, :, _PAIR_ROW[1:]]], axis=-1)


def _na_bias_kernel(t_ref, o_ref):
    kc = lax.broadcasted_iota(jnp.int32, (N_PAIR, LANES), 1) & (GRID_W - 1)
    for h in range(NA_HEADS):
        t = t_ref[h] * LOG2E
        for c in range(GRID_W):
            ws = min(max(c - NA_WIN_COLS // 2, 0), GRID_W - NA_WIN_COLS)
            rolled = pltpu.roll(t, (LANES - (NA_WIN_COLS - 1) + c) % LANES, 1)
            o_ref[h, :, c, :] = jnp.where(jnp.logical_and(kc >= ws, kc < ws + NA_WIN_COLS), rolled, NEG)


def _in_perm_kernel(w_ref, o_ref):
    lane = lax.broadcasted_iota(jnp.int32, (w_ref.shape[0], LANES), 1)
    piece = 32
    for oc in range(D_PROJ // LANES):
        starts = [int(_IN_PERM[oc * LANES + piece * p]) for p in range(LANES // piece)]
        if all(s == starts[0] + piece * p for p, s in enumerate(starts)) and starts[0] % LANES == 0:
            acc = w_ref[:, starts[0]:starts[0] + LANES]
        else:
            acc = None
            for p, s in enumerate(starts):
                chunk, off = divmod(s, LANES)
                v = w_ref[:, chunk * LANES:(chunk + 1) * LANES]
                shift = (piece * p - off) % LANES
                v = pltpu.roll(v, shift, 1) if shift else v
                acc = v if acc is None else jnp.where(lane >= piece * p, v, acc)
        o_ref[:, oc * LANES:(oc + 1) * LANES] = acc.astype(BF16)


def _mod_kernel(c_ref, w_ref, b_ref, o_ref):
    c = c_ref[...]
    act = (c * jax.nn.sigmoid(c)).astype(BF16)
    o_ref[...] = jnp.dot(act, w_ref[...].astype(BF16), preferred_element_type=F32) + b_ref[...]


def _norm_mod(x, gain, shift, scale):
    y = x * lax.rsqrt(jnp.mean(x * x, axis=-1, keepdims=True) + EPS)
    return (y * gain) * (1 + scale) + shift


def _qkv_kernel(x_ref, g_ref, mod_ref, cmod_ref, w_ref, cos_ref, sin_ref, qg_ref, kg_ref,
                qb_ref, qa_ref, qw_ref, kb_ref, vba_ref, vbb_ref, kw_ref, vwa_ref, vwb_ref,
                ka_ref, vaa_ref, vab_ref):
    sub = QKV_SUB
    row0 = pl.program_id(1) * QKV_TM
    lane = lax.broadcasted_iota(jnp.int32, (sub, LANES), 1)
    head_a = ((lane >> 5) & 1) == 0
    lane_lo = lane < HEAD_DIM
    qscale = HEAD_DIM ** -0.5 * LOG2E

    def headnorm(xc, gain):
        sq = xc * xc
        sa = jnp.sum(jnp.where(head_a, sq, 0.0), axis=-1, keepdims=True)
        sb = jnp.sum(jnp.where(head_a, 0.0, sq), axis=-1, keepdims=True)
        ms = jnp.where(head_a, sa, sb) * (1.0 / HEAD_DIM)
        return (xc * lax.rsqrt(ms + EPS)) * gain

    def epilogue(y, rows):
        cosv, sinv = cos_ref[rows, :], sin_ref[rows, :]

        def rope(xc):
            return xc * cosv + pltpu.roll(xc, 64, 1) * sinv

        def chunk(c0):
            return y[:, c0:c0 + LANES]

        def put_v(va_ref, vb_ref, c, vc):
            va_ref[rows, c * LANES:(c + 1) * LANES] = jnp.where(lane_lo, vc, 1.0).astype(BF16)
            vb_ref[rows, c * LANES:(c + 1) * LANES] = jnp.where(lane_lo, 1.0, vc).astype(BF16)

        for c in range(4):
            qb_ref[rows, c * LANES:(c + 1) * LANES] = (
                rope(headnorm(chunk(c * LANES), qg_ref[...])) * qscale).astype(BF16)
        qa_ref[rows, :] = (y[:, 512:768] * qscale).astype(BF16)
        ka_ref[rows, :] = y[:, 768:1024].astype(BF16)
        for c in range(2):
            put_v(vaa_ref, vab_ref, c, chunk(1024 + c * LANES))
            qw_ref[rows, c * LANES:(c + 1) * LANES] = (
                rope(chunk(1280 + c * LANES)) * qscale).astype(BF16)
        kb_ref[rows, :] = rope(headnorm(chunk(1536), kg_ref[...])).astype(BF16)
        put_v(vba_ref, vbb_ref, 0, chunk(1664))
        kw_ref[rows, :] = rope(chunk(1792)).astype(BF16)
        put_v(vwa_ref, vwb_ref, 0, chunk(1920))

    pending = None
    for s in range(QKV_TM // sub):
        rows = slice(s * sub, (s + 1) * sub)
        is_ctx = row0 + s * sub >= SEQ
        shift = jnp.where(is_ctx, cmod_ref[0:1, :], mod_ref[0:1, :])
        scale = jnp.where(is_ctx, cmod_ref[1:2, :], mod_ref[1:2, :])
        h = _norm_mod(x_ref[rows, :], g_ref[...], shift, scale)
        y = jnp.dot(h.astype(BF16), w_ref[...], preferred_element_type=F32)
        if pending is not None:
            epilogue(*pending)
        pending = (y, rows)
    epilogue(*pending)


def _qk(q, k):
    return lax.dot_general(q, k, (((1,), (1,)), ((), ())), preferred_element_type=F32)


def _softmax_pv(parts, extra=None):
    m = None
    for s, _ in parts:
        sm = jnp.max(s, axis=-1, keepdims=True)
        m = sm if m is None else jnp.maximum(m, sm)
    if extra is not None:
        m = jnp.maximum(m, extra)
    o = None
    for s, v in parts:
        pv = jnp.dot(jnp.exp2(s - m).astype(BF16), v, preferred_element_type=F32)
        o = pv if o is None else o + pv
    den = pltpu.roll(o, HEAD_DIM, 1)
    if extra is not None:
        den = den + jnp.exp2(extra - m)
    return o / den


def _split_heads(q, a_mask):
    qf = q.astype(F32)
    return jnp.where(a_mask, qf, 0.0).astype(BF16), jnp.where(a_mask, 0.0, qf).astype(BF16)


def _head_tasks(rows, col, q_ref, c, a_mask, score):
    def task(w):
        return score(_split_heads(q_ref[rows, c * LANES:(c + 1) * LANES], a_mask)[w], w)
    return [(rows, col, w, functools.partial(task, w)) for w in range(2)]


def _run_heads(tasks, att_ref):
    lane_lo = lax.broadcasted_iota(jnp.int32, (TM, LANES), 1) < HEAD_DIM
    held = {}
    queue = [task[3]() for task in tasks[:ATTN_AHEAD]]
    for t, (rows, col, which, _) in enumerate(tasks):
        if t + ATTN_AHEAD < len(tasks):
            queue.append(tasks[t + ATTN_AHEAD][3]())
        o = _softmax_pv(*queue.pop(0))
        if which == 0:
            held[(rows.start, col)] = o
        else:
            att_ref[rows, col:col + LANES] = jnp.where(
                lane_lo, held.pop((rows.start, col)), o).astype(BF16)


def _attn_kernel(sink_ref, x_ref, mod_ref, wo_ref, bias_ref, qb_ref, qa_ref, qw_ref,
                 kb_ref, vba_ref, vbb_ref, kw_ref, vwa_ref, vwb_ref, ka_ref, vaa_ref, vab_ref,
                 o_ref, att_ref):
    lane = lax.broadcasted_iota(jnp.int32, (TM, LANES), 1)
    rope_a = ((lane >> 5) & 1) == 0
    lane_lo = lane < HEAD_DIM
    ctx = slice(SEQ, ROWS)
    vb_refs, vw_refs, va_refs = (vba_ref, vbb_ref), (vwa_ref, vwb_ref), (vaa_ref, vab_ref)

    tasks = []
    for sb in range(ATT_TM // TM):
        j = pl.program_id(1) * (ATT_TM // TM) + sb
        rows = slice(sb * TM, (sb + 1) * TM)

        for c in range(4):
            tasks += _head_tasks(rows, 256 + c * LANES, qb_ref, c, rope_a,
                                 lambda q_h, w: ([(_qk(q_h, kb_ref[...]), vb_refs[w][...])], None))

        start = pl.multiple_of(jnp.clip(j * TM - WA_RADIUS, 0, SEQ - WA_SPAN), WA_RADIUS)
        loc = pl.ds(start, WA_SPAN)
        qpos = j * TM + lax.broadcasted_iota(jnp.int32, (TM, 1), 0)
        kpos = start + lax.broadcasted_iota(jnp.int32, (1, WA_SPAN), 1)
        near = jnp.abs(qpos - kpos) <= WA_RADIUS
        for c in range(2):
            def w_score(q_h, w, c=c, loc=loc, near=near):
                return ([(jnp.where(near, _qk(q_h, kw_ref[loc, :]), NEG), vw_refs[w][loc, :]),
                         (_qk(q_h, kw_ref[ctx, :]), vw_refs[w][ctx, :])],
                        sink_ref[c + 2 * w] * LOG2E)
            tasks += _head_tasks(rows, 768 + c * LANES, qw_ref, c, rope_a, w_score)

        row0 = j * NA_BLK_ROWS
        span0 = jnp.clip(row0 - NA_WIN_ROWS // 2, 0, GRID_H - NA_SPAN_ROWS)
        nloc = pl.ds(pl.multiple_of(span0 * GRID_W, GRID_W), NA_SPAN)
        qrow = row0 + (lax.broadcasted_iota(jnp.int32, (TM, 1), 0) >> 6)
        wrow = jnp.clip(qrow - NA_WIN_ROWS // 2, 0, GRID_H - NA_WIN_ROWS)
        krow = span0 + (lax.broadcasted_iota(jnp.int32, (1, NA_SPAN), 1) >> 6)
        in_rows = jnp.logical_and(krow >= wrow, krow < wrow + NA_WIN_ROWS)
        for c in range(2):
            def a_score(q_h, w, c=c, row0=row0, span0=span0, nloc=nloc, in_rows=in_rows):
                cs = slice(c * LANES, (c + 1) * LANES)
                bias = jnp.concatenate(
                    [jnp.concatenate(
                        [bias_ref[2 * c + w,
                                  jnp.clip(span0 + 2 * p - (row0 + iq) + NA_WIN_ROWS, 0, N_PAIR - 1)]
                         for p in range(NA_SPAN_ROWS // 2)], axis=1)
                     for iq in range(NA_BLK_ROWS)], axis=0)
                s_loc = jnp.where(in_rows, _qk(q_h, ka_ref[nloc, cs]) + bias, NEG)
                return ([(s_loc, va_refs[w][nloc, cs]),
                         (_qk(q_h, ka_ref[ctx, cs]), va_refs[w][ctx, cs])], None)
            tasks += _head_tasks(rows, c * LANES, qa_ref, c, lane_lo, a_score)

    _run_heads(tasks, att_ref)
    o = jnp.dot(att_ref[...], wo_ref[...], preferred_element_type=F32)
    o_ref[...] = x_ref[...] + mod_ref[2:3, :] * o


def _ctx_attn_kernel(sink_ref, x_ref, mod_ref, wo_ref, qb_ref, qa_ref, qw_ref,
                     kb_ref, vba_ref, vbb_ref, kw_ref, vwa_ref, vwb_ref, ka_ref, vaa_ref, vab_ref,
                     o_ref, att_ref):
    lane = lax.broadcasted_iota(jnp.int32, (TM, LANES), 1)
    rope_a = ((lane >> 5) & 1) == 0
    lane_lo = lane < HEAD_DIM

    def attend(q_ref, chunks, a_mask, k, v_a, v_b, col, extra=None):
        halves = [_split_heads(q_ref[:, c * LANES:(c + 1) * LANES], a_mask) for c in chunks]
        n = len(chunks) * TM
        s = _qk(jnp.concatenate([h[0] for h in halves] + [h[1] for h in halves], axis=0), k)
        m = jnp.max(s, axis=-1, keepdims=True)
        if extra is not None:
            m = jnp.maximum(m, extra)
        p = jnp.exp2(s - m).astype(BF16)
        o = jnp.concatenate([jnp.dot(p[:n], v_a, preferred_element_type=F32),
                             jnp.dot(p[n:], v_b, preferred_element_type=F32)], axis=0)
        den = pltpu.roll(o, HEAD_DIM, 1)
        if extra is not None:
            den = den + jnp.exp2(extra - m)
        o = o / den
        for i in range(len(chunks)):
            att_ref[:, col + i * LANES:col + (i + 1) * LANES] = jnp.where(
                lane_lo, o[i * TM:(i + 1) * TM], o[n + i * TM:n + (i + 1) * TM]).astype(BF16)

    attend(qb_ref, range(4), rope_a, kb_ref[...], vba_ref[...], vbb_ref[...], 256)
    sinks = jnp.concatenate([jnp.full((TM, 1), sink_ref[h] * LOG2E, F32) for h in range(WA_HEADS)], axis=0)
    attend(qw_ref, range(2), rope_a, kw_ref[...], vwa_ref[...], vwb_ref[...], 768, extra=sinks)
    for c in range(2):
        cs = slice(c * LANES, (c + 1) * LANES)
        attend(qa_ref, [c], lane_lo, ka_ref[:, cs], vaa_ref[:, cs], vab_ref[:, cs], c * LANES)
    o = jnp.dot(att_ref[...], wo_ref[...], preferred_element_type=F32)
    o_ref[...] = x_ref[...] + mod_ref[2:3, :] * o


def _ffn_kernel(x_ref, xc_ref, xp_ref, xn_ref, g_ref, mod_ref, wup_ref, cw_ref, cb_ref, wdn_ref,
                fg_ref, o_ref, *, final):
    j = pl.program_id(1)
    xm = x_ref[...] if final else jnp.where(j < N_LAT, x_ref[...], xc_ref[...])
    xf = jnp.concatenate([xp_ref[...], xm, xn_ref[...]], axis=0)
    h = _norm_mod(xf, g_ref[...], mod_ref[3:4, :], mod_ref[4:5, :])
    af = jnp.dot(h.astype(BF16), wup_ref[:, :D_FF], preferred_element_type=F32)
    b = jnp.dot(h[HALO:HALO + TM].astype(BF16), wup_ref[:, D_FF:], preferred_element_type=F32)
    a_mid = af[HALO:HALO + TM]
    has_prev = jnp.logical_and(j != 0, j != N_LAT)
    has_next = jnp.logical_and(j != N_LAT - 1, j != N_BLK - 1)
    a_prev = jnp.where(has_prev, af[HALO - 1:HALO], 0.0)
    a_next = jnp.where(has_next, af[HALO + TM:HALO + TM + 1], 0.0)
    row = lax.broadcasted_iota(jnp.int32, (TM, 1), 0)
    a_up = jnp.where(row == 0, a_prev, pltpu.roll(a_mid, 1, 0))
    a_dn = jnp.where(row == TM - 1, a_next, pltpu.roll(a_mid, TM - 1, 0))
    a = a_up * cw_ref[0:1, :] + a_mid * cw_ref[1:2, :] + a_dn * cw_ref[2:3, :] + cb_ref[...]
    g = (a * jax.nn.sigmoid(a)) * b
    y = jnp.dot(g.astype(BF16), wdn_ref[...], preferred_element_type=F32)
    out = xm + mod_ref[5:6, :] * y
    if final:
        out = (out * lax.rsqrt(jnp.mean(out * out, axis=-1, keepdims=True) + EPS)) * fg_ref[...]
    o_ref[...] = out


def _row_spec(width, rows=TM):
    return pl.BlockSpec((None, rows, width), lambda b, j: (b, j, 0))


def _full_spec(width):
    return pl.BlockSpec((None, ROWS, width), lambda b, j: (b, 0, 0))


def _const_spec(shape):
    nd = len(shape)
    return pl.BlockSpec(shape, lambda b, j: (0,) * nd)


def _layer_spec(l, shape):
    nd = len(shape)
    return pl.BlockSpec((None,) + tuple(shape), lambda b, j: (l,) + (0,) * nd)


def _mod_spec(l):
    return pl.BlockSpec((None, None, N_MOD, D_MODEL),
                        lambda b, j: (l, jnp.where(j < N_LAT, b, BATCH), 0, 0))


def _act_shape(width, dtype=BF16):
    return jax.ShapeDtypeStruct((BATCH, ROWS, width), dtype)


def kernel(x, c, ctx, c_ctx, attn_norm, ffn_norm, w_mod, b_mod, w_in, q_gain, k_gain,
           na_rpb, wa_sink, w_out, w_up, conv_w, conv_b, w_down, final_norm):
    xa = jnp.concatenate([x, ctx], axis=1)
    c_all = jnp.zeros((MOD_ROWS, D_MODEL), F32).at[:BATCH].set(c).at[BATCH].set(c_ctx)
    w_out_p = _take_runs(w_out.astype(BF16), _OUT_PERM, 1)
    w_up_b = w_up.astype(BF16)
    w_dn_b = w_down.astype(BF16)
    qg = q_gain[:, _GAIN_PERM].reshape(DEPTH, 1, LANES)
    kg = k_gain[:, _GAIN_PERM].reshape(DEPTH, 1, LANES)
    cos128, sin128 = _rope_tables()
    an = attn_norm.reshape(DEPTH, 1, D_MODEL)
    fn = ffn_norm.reshape(DEPTH, 1, D_MODEL)
    fg = final_norm.reshape(1, D_MODEL)
    cb = conv_b.reshape(DEPTH, 1, D_FF)
    bm = b_mod.reshape(DEPTH, 1, N_MOD * D_MODEL)

    bias_tab = pl.pallas_call(
        _na_bias_kernel,
        grid=(DEPTH,),
        in_specs=[pl.BlockSpec((None, NA_HEADS, N_PAIR, LANES), lambda l: (l, 0, 0, 0))],
        out_specs=pl.BlockSpec((None, NA_HEADS, N_PAIR, GRID_W, LANES), lambda l: (l, 0, 0, 0, 0)),
        out_shape=jax.ShapeDtypeStruct((DEPTH, NA_HEADS, N_PAIR, GRID_W, LANES), F32),
        compiler_params=_params(1),
        name="nbr_bias",
    )(_na_bias_rows(na_rpb))

    w_in_p = pl.pallas_call(
        _in_perm_kernel,
        grid=(DEPTH, D_MODEL // TM),
        in_specs=[pl.BlockSpec((None, TM, D_PROJ), lambda l, k: (l, k, 0))],
        out_specs=pl.BlockSpec((None, TM, D_PROJ), lambda l, k: (l, k, 0)),
        out_shape=jax.ShapeDtypeStruct((DEPTH, D_MODEL, D_PROJ), BF16),
        compiler_params=_params(2),
        name="w_in_perm",
    )(w_in)

    mods = pl.pallas_call(
        _mod_kernel,
        grid=(DEPTH, N_MOD),
        in_specs=[pl.BlockSpec((MOD_ROWS, D_MODEL), lambda l, n: (0, 0)),
                  pl.BlockSpec((None, D_MODEL, D_MODEL), lambda l, n: (l, 0, n)),
                  pl.BlockSpec((None, 1, D_MODEL), lambda l, n: (l, 0, n))],
        out_specs=pl.BlockSpec((None, MOD_ROWS, D_MODEL), lambda l, n: (l, 0, n)),
        out_shape=jax.ShapeDtypeStruct((DEPTH, MOD_ROWS, N_MOD * D_MODEL), F32),
        compiler_params=_params(2),
        name="adaln_mod",
    )(c_all, w_mod, bm)
    mods = mods.reshape(DEPTH, MOD_ROWS, N_MOD, D_MODEL)

    qkv_widths = (512, 256, 256, 128, 128, 128, 128, 128, 128, 256, 256, 256)
    for l in range(DEPTH):
        last = l == DEPTH - 1
        nq = N_LAT if last else N_BLK
        out_rows = SEQ if last else ROWS

        qkv = pl.pallas_call(
            _qkv_kernel,
            grid=(BATCH, ROWS // QKV_TM),
            in_specs=[_row_spec(D_MODEL, QKV_TM), _layer_spec(l, (1, D_MODEL)),
                      pl.BlockSpec((None, None, N_MOD, D_MODEL), lambda b, j: (l, b, 0, 0)),
                      pl.BlockSpec((None, None, N_MOD, D_MODEL), lambda b, j: (l, BATCH, 0, 0)),
                      _layer_spec(l, (D_MODEL, D_PROJ)),
                      pl.BlockSpec((QKV_TM, LANES), lambda b, j: (j, 0)),
                      pl.BlockSpec((QKV_TM, LANES), lambda b, j: (j, 0)),
                      _layer_spec(l, (1, LANES)), _layer_spec(l, (1, LANES))],
            out_specs=[_row_spec(w, QKV_TM) for w in qkv_widths],
            out_shape=[_act_shape(w) for w in qkv_widths],
            compiler_params=_params(2, VMEM_LIMIT),
            name=f"qkv_proj_{l}",
        )(xa, an, mods, mods, w_in_p, cos128, sin128, qg, kg)

        x1 = pl.pallas_call(
            _attn_kernel,
            grid=(BATCH, SEQ // ATT_TM),
            in_specs=[pl.BlockSpec(memory_space=pltpu.SMEM),
                      _row_spec(D_MODEL, ATT_TM), _mod_spec(l), _layer_spec(l, (D_MODEL, D_MODEL)),
                      _layer_spec(l, (NA_HEADS, N_PAIR, GRID_W, LANES)),
                      _row_spec(512, ATT_TM), _row_spec(256, ATT_TM), _row_spec(256, ATT_TM)]
                     + [_full_spec(w) for w in qkv_widths[3:]],
            out_specs=_row_spec(D_MODEL, ATT_TM),
            out_shape=jax.ShapeDtypeStruct((BATCH, SEQ, D_MODEL), F32),
            scratch_shapes=[pltpu.VMEM((ATT_TM, D_MODEL), BF16)],
            compiler_params=_params(2, VMEM_LIMIT),
            name=f"attn_{l}",
        )(wa_sink[l], xa, mods, w_out_p, bias_tab, *qkv)

        if last:
            x1c = x1
        else:
            ctx_spec = lambda w: pl.BlockSpec((None, TM, w), lambda b: (b, N_LAT, 0))
            x1c = pl.pallas_call(
                _ctx_attn_kernel,
                grid=(BATCH,),
                in_specs=[pl.BlockSpec(memory_space=pltpu.SMEM), ctx_spec(D_MODEL),
                          pl.BlockSpec((None, None, N_MOD, D_MODEL), lambda b: (l, BATCH, 0, 0)),
                          pl.BlockSpec((None, D_MODEL, D_MODEL), lambda b: (l, 0, 0))]
                         + [ctx_spec(w) for w in qkv_widths],
                out_specs=pl.BlockSpec((None, CTX_LEN, D_MODEL), lambda b: (b, 0, 0)),
                out_shape=jax.ShapeDtypeStruct((BATCH, CTX_LEN, D_MODEL), F32),
                scratch_shapes=[pltpu.VMEM((TM, D_MODEL), BF16)],
                compiler_params=_params(1, VMEM_LIMIT),
                name=f"ctx_attn_{l}",
            )(wa_sink[l], xa, mods, w_out_p, *qkv)

        blk8 = TM // HALO
        n_halo = SEQ // HALO
        xa = pl.pallas_call(
            functools.partial(_ffn_kernel, final=last),
            grid=(BATCH, nq),
            in_specs=[pl.BlockSpec((None, TM, D_MODEL),
                                   lambda b, j: (b, jnp.minimum(j, N_LAT - 1), 0)),
                      pl.BlockSpec((None, CTX_LEN, D_MODEL), lambda b, j: (b, 0, 0)),
                      pl.BlockSpec((None, HALO, D_MODEL),
                                   lambda b, j: (b, jnp.maximum(j * blk8 - 1, 0), 0)),
                      pl.BlockSpec((None, HALO, D_MODEL),
                                   lambda b, j: (b, jnp.minimum((j + 1) * blk8, n_halo - 1), 0)),
                      _layer_spec(l, (1, D_MODEL)), _mod_spec(l),
                      _layer_spec(l, (D_MODEL, 2 * D_FF)),
                      _layer_spec(l, (CONV_W, D_FF)), _layer_spec(l, (1, D_FF)),
                      _layer_spec(l, (D_FF, D_MODEL)), _const_spec((1, D_MODEL))],
            out_specs=_row_spec(D_MODEL),
            out_shape=jax.ShapeDtypeStruct((BATCH, out_rows, D_MODEL), F32),
            compiler_params=_params(2, VMEM_LIMIT),
            name=f"conv_ffn_{l}",
        )(x1, x1c, x1, x1, fn, mods, w_up_b, conv_w, cb, w_dn_b, fg)

    return xa
```

```python
import functools
import math

import numpy as np
import jax
import jax.numpy as jnp
from jax import lax
from jax.experimental import pallas as pl
from jax.experimental.pallas import tpu as pltpu

D_MODEL = 1024
BATCH = 8
SEQ = 2048
DEPTH = 4
GRID_W = 64
CTX_LEN = 256
HEAD_DIM = 64
NA_HEADS = 4
GA_HEADS = 8
GA_KV_HEADS = 2
WA_HEADS = 4
WA_KV_HEADS = 2
NA_WIN_ROWS = 8
NA_WIN_COLS = 16
WA_RADIUS = 128
D_FF = 2816
CONV_W = 3
ROPE_THETA = 10000.0
EPS = 1e-6
N_MOD = 6
D_PROJ = 2048

ROWS = SEQ + CTX_LEN
TM = 256
N_BLK = ROWS // TM
N_LAT = SEQ // TM
GRID_H = SEQ // GRID_W
LANES = 128
HALO = 8
MOD_ROWS = 16
NEG = -1e30
LOG2E = math.log2(math.e)
VMEM_LIMIT = 56 * 1024 * 1024

WA_SPAN = TM + 2 * WA_RADIUS
NA_BLK_ROWS = TM // GRID_W
NA_SPAN_ROWS = 12
NA_SPAN = NA_SPAN_ROWS * GRID_W
N_DR = 2 * NA_WIN_ROWS - 1
N_DC = 2 * NA_WIN_COLS - 1
N_PAIR = 16
ATT_TM = 512
ATTN_AHEAD = 1
QKV_SUB = 256

F32 = jnp.float32
BF16 = jnp.bfloat16


def _params(n_axes, vmem=None):
    return pltpu.CompilerParams(dimension_semantics=("arbitrary",) * n_axes,
                                vmem_limit_bytes=vmem)


def _rope_lane_perm(base, head_stride, n_chunks):
    cols = []
    for c in range(n_chunks):
        for lane in range(LANES):
            part, hsel, i = lane // 64, (lane // 32) % 2, lane % 32
            cols.append(base + HEAD_DIM * (c + head_stride * hsel) + 32 * part + i)
    return cols


def _in_proj_perm():
    qa, ka, va = 0, 256, 512
    qb, kb, vb = 768, 1280, 1408
    qw, kw, vw = 1536, 1792, 1920
    cols = []
    cols += _rope_lane_perm(qb, 4, 4)
    cols += list(range(qa, qa + 256))
    cols += list(range(ka, ka + 256))
    cols += list(range(va, va + 256))
    cols += _rope_lane_perm(qw, 2, 2)
    cols += _rope_lane_perm(kb, 1, 1)
    cols += list(range(vb, vb + 128))
    cols += _rope_lane_perm(kw, 1, 1)
    cols += list(range(vw, vw + 128))
    return np.asarray(cols, np.int32)


def _out_proj_perm():
    rows = list(range(256))
    for c in range(4):
        for lane in range(LANES):
            rows.append(256 + HEAD_DIM * (c + 4 * (lane // 64)) + lane % 64)
    for c in range(2):
        for lane in range(LANES):
            rows.append(768 + HEAD_DIM * (c + 2 * (lane // 64)) + lane % 64)
    return np.asarray(rows, np.int32)


_IN_PERM = _in_proj_perm()
_OUT_PERM = _out_proj_perm()
_GAIN_PERM = np.asarray([32 * (l // 64) + l % 32 for l in range(LANES)], np.int32)
_PAIR_ROW = np.clip(np.arange(N_PAIR + 1) - 1, 0, N_DR - 1)


def _take_runs(a, perm, axis):
    cuts = [0] + [i for i in range(1, len(perm)) if perm[i] != perm[i - 1] + 1] + [len(perm)]
    return jnp.concatenate(
        [lax.slice_in_dim(a, int(perm[s]), int(perm[e - 1]) + 1, axis=axis)
         for s, e in zip(cuts[:-1], cuts[1:])], axis=axis)


def _rope_tables():
    t = jnp.arange(SEQ, dtype=jnp.int32)
    row = (t // GRID_W).astype(F32)
    col = (t % GRID_W).astype(F32)
    n = HEAD_DIM // 4
    inv = ROPE_THETA ** (-jnp.arange(n, dtype=F32) / n)
    ang = jnp.concatenate([row[:, None] * inv, col[:, None] * inv], axis=-1)
    cos, sin = jnp.cos(ang), jnp.sin(ang)
    cos128 = jnp.concatenate([cos, cos, cos, cos], axis=-1)
    sin128 = jnp.concatenate([-sin, -sin, sin, sin], axis=-1)
    cos128 = jnp.concatenate([cos128, jnp.ones((CTX_LEN, LANES), F32)], axis=0)
    sin128 = jnp.concatenate([sin128, jnp.zeros((CTX_LEN, LANES), F32)], axis=0)
    return cos128, sin128


def _na_bias_rows(na_rpb):
    rp = jnp.pad(na_rpb, ((0, 0), (0, 0), (0, 0), (0, GRID_W - N_DC)))
    return jnp.concatenate([rp[:, :, _PAIR_ROW[:-1]], rp[:, :, _PAIR_ROW[1:]]], axis=-1)


def _na_bias_kernel(t_ref, o_ref):
    kc = lax.broadcasted_iota(jnp.int32, (N_PAIR, LANES), 1) & (GRID_W - 1)
    for h in range(NA_HEADS):
        t = t_ref[h] * LOG2E
        for c in range(GRID_W):
            ws = min(max(c - NA_WIN_COLS // 2, 0), GRID_W - NA_WIN_COLS)
            rolled = pltpu.roll(t, (LANES - (NA_WIN_COLS - 1) + c) % LANES, 1)
            o_ref[h, :, c, :] = jnp.where(jnp.logical_and(kc >= ws, kc < ws + NA_WIN_COLS), rolled, NEG)


def _in_perm_kernel(w_ref, o_ref):
    lane = lax.broadcasted_iota(jnp.int32, (w_ref.shape[0], LANES), 1)
    piece = 32
    for oc in range(D_PROJ // LANES):
        starts = [int(_IN_PERM[oc * LANES + piece * p]) for p in range(LANES // piece)]
        if all(s == starts[0] + piece * p for p, s in enumerate(starts)) and starts[0] % LANES == 0:
            acc = w_ref[:, starts[0]:starts[0] + LANES]
        else:
            acc = None
            for p, s in enumerate(starts):
                chunk, off = divmod(s, LANES)
                v = w_ref[:, chunk * LANES:(chunk + 1) * LANES]
                shift = (piece * p - off) % LANES
                v = pltpu.roll(v, shift, 1) if shift else v
                acc = v if acc is None else jnp.where(lane >= piece * p, v, acc)
        o_ref[:, oc * LANES:(oc + 1) * LANES] = acc.astype(BF16)


def _mod_kernel(c_ref, w_ref, b_ref, o_ref):
    c = c_ref[...]
    act = (c * jax.nn.sigmoid(c)).astype(BF16)
    o_ref[...] = jnp.dot(act, w_ref[...].astype(BF16), preferred_element_type=F32) + b_ref[...]


def _norm_mod(x, gain, shift, scale):
    y = x * lax.rsqrt(jnp.mean(x * x, axis=-1, keepdims=True) + EPS)
    return (y * gain) * (1 + scale) + shift


def _qkv_kernel(x_ref, xc_ref, g_ref, mod_ref, cmod_ref, w_ref, cos_ref, sin_ref, qg_ref, kg_ref,
                qb_ref, qa_ref, qw_ref, kb_ref, vba_ref, vbb_ref, kw_ref, vwa_ref, vwb_ref,
                ka_ref, vaa_ref, vab_ref):
    sub = QKV_SUB
    lane = lax.broadcasted_iota(jnp.int32, (sub, LANES), 1)
    head_a = ((lane >> 5) & 1) == 0
    lane_lo = lane < HEAD_DIM
    qscale = HEAD_DIM ** -0.5 * LOG2E

    def headnorm(xc, gain):
        sq = xc * xc
        sa = jnp.sum(jnp.where(head_a, sq, 0.0), axis=-1, keepdims=True)
        sb = jnp.sum(jnp.where(head_a, 0.0, sq), axis=-1, keepdims=True)
        ms = jnp.where(head_a, sa, sb) * (1.0 / HEAD_DIM)
        return (xc * lax.rsqrt(ms + EPS)) * gain

    def epilogue(y, rows):
        cosv, sinv = cos_ref[rows, :], sin_ref[rows, :]

        def rope(xc):
            return xc * cosv + pltpu.roll(xc, 64, 1) * sinv

        def chunk(c0):
            return y[:, c0:c0 + LANES]

        def put_v(va_ref, vb_ref, c, vc):
            va_ref[rows, c * LANES:(c + 1) * LANES] = jnp.where(lane_lo, vc, 1.0).astype(BF16)
            vb_ref[rows, c * LANES:(c + 1) * LANES] = jnp.where(lane_lo, 1.0, vc).astype(BF16)

        for c in range(4):
            qb_ref[rows, c * LANES:(c + 1) * LANES] = (
                rope(headnorm(chunk(c * LANES), qg_ref[...])) * qscale).astype(BF16)
        qa_ref[rows, :] = (y[:, 512:768] * qscale).astype(BF16)
        ka_ref[rows, :] = y[:, 768:1024].astype(BF16)
        for c in range(2):
            put_v(vaa_ref, vab_ref, c, chunk(1024 + c * LANES))
            qw_ref[rows, c * LANES:(c + 1) * LANES] = (
                rope(chunk(1280 + c * LANES)) * qscale).astype(BF16)
        kb_ref[rows, :] = rope(headnorm(chunk(1536), kg_ref[...])).astype(BF16)
        put_v(vba_ref, vbb_ref, 0, chunk(1664))
        kw_ref[rows, :] = rope(chunk(1792)).astype(BF16)
        put_v(vwa_ref, vwb_ref, 0, chunk(1920))

    pending = None
    for s in range(ROWS // sub):
        rows = slice(s * sub, (s + 1) * sub)
        if s * sub < SEQ:
            xs, m_ref = x_ref[rows, :], mod_ref
        else:
            xs, m_ref = xc_ref[s * sub - SEQ:(s + 1) * sub - SEQ, :], cmod_ref
        h = _norm_mod(xs, g_ref[...], m_ref[0:1, :], m_ref[1:2, :])
        y = jnp.dot(h.astype(BF16), w_ref[...], preferred_element_type=F32)
        if pending is not None:
            epilogue(*pending)
        pending = (y, rows)
    epilogue(*pending)


def _qk(q, k):
    return lax.dot_general(q, k, (((1,), (1,)), ((), ())), preferred_element_type=F32)


def _softmax_pv(parts, extra=None):
    m = None
    for s, _ in parts:
        sm = jnp.max(s, axis=-1, keepdims=True)
        m = sm if m is None else jnp.maximum(m, sm)
    if extra is not None:
        m = jnp.maximum(m, extra)
    o = None
    for s, v in parts:
        pv = jnp.dot(jnp.exp2(s - m).astype(BF16), v, preferred_element_type=F32)
        o = pv if o is None else o + pv
    den = pltpu.roll(o, HEAD_DIM, 1)
    if extra is not None:
        den = den + jnp.exp2(extra - m)
    return o / den


def _split_heads(q, a_mask):
    qf = q.astype(F32)
    return jnp.where(a_mask, qf, 0.0).astype(BF16), jnp.where(a_mask, 0.0, qf).astype(BF16)


def _head_tasks(rows, col, q_ref, c, a_mask, score):
    def task(w):
        return score(_split_heads(q_ref[rows, c * LANES:(c + 1) * LANES], a_mask)[w], w)
    return [(rows, col, w, functools.partial(task, w)) for w in range(2)]


def _run_heads(tasks, att_ref):
    lane_lo = lax.broadcasted_iota(jnp.int32, (TM, LANES), 1) < HEAD_DIM
    held = {}
    queue = [task[3]() for task in tasks[:ATTN_AHEAD]]
    for t, (rows, col, which, _) in enumerate(tasks):
        if t + ATTN_AHEAD < len(tasks):
            queue.append(tasks[t + ATTN_AHEAD][3]())
        o = _softmax_pv(*queue.pop(0))
        if which == 0:
            held[(rows.start, col)] = o
        else:
            att_ref[rows, col:col + LANES] = jnp.where(
                lane_lo, held.pop((rows.start, col)), o).astype(BF16)


def _attn_kernel(sink_ref, x_ref, mod_ref, wo_ref, bias_ref, qb_ref, qa_ref, qw_ref,
                 kb_ref, vba_ref, vbb_ref, kw_ref, vwa_ref, vwb_ref, ka_ref, vaa_ref, vab_ref,
                 o_ref, att_ref):
    lane = lax.broadcasted_iota(jnp.int32, (TM, LANES), 1)
    rope_a = ((lane >> 5) & 1) == 0
    lane_lo = lane < HEAD_DIM
    ctx = slice(SEQ, ROWS)
    vb_refs, vw_refs, va_refs = (vba_ref, vbb_ref), (vwa_ref, vwb_ref), (vaa_ref, vab_ref)

    tasks = []
    for sb in range(ATT_TM // TM):
        j = pl.program_id(1) * (ATT_TM // TM) + sb
        rows = slice(sb * TM, (sb + 1) * TM)

        for c in range(4):
            tasks += _head_tasks(rows, 256 + c * LANES, qb_ref, c, rope_a,
                                 lambda q_h, w: ([(_qk(q_h, kb_ref[...]), vb_refs[w][...])], None))

        start = pl.multiple_of(jnp.clip(j * TM - WA_RADIUS, 0, SEQ - WA_SPAN), WA_RADIUS)
        loc = pl.ds(start, WA_SPAN)
        qpos = j * TM + lax.broadcasted_iota(jnp.int32, (TM, 1), 0)
        kpos = start + lax.broadcasted_iota(jnp.int32, (1, WA_SPAN), 1)
        near = jnp.abs(qpos - kpos) <= WA_RADIUS
        for c in range(2):
            def w_score(q_h, w, c=c, loc=loc, near=near):
                return ([(jnp.where(near, _qk(q_h, kw_ref[loc, :]), NEG), vw_refs[w][loc, :]),
                         (_qk(q_h, kw_ref[ctx, :]), vw_refs[w][ctx, :])],
                        sink_ref[c + 2 * w] * LOG2E)
            tasks += _head_tasks(rows, 768 + c * LANES, qw_ref, c, rope_a, w_score)

        row0 = j * NA_BLK_ROWS
        span0 = jnp.clip(row0 - NA_WIN_ROWS // 2, 0, GRID_H - NA_SPAN_ROWS)
        nloc = pl.ds(pl.multiple_of(span0 * GRID_W, GRID_W), NA_SPAN)
        qrow = row0 + (lax.broadcasted_iota(jnp.int32, (TM, 1), 0) >> 6)
        wrow = jnp.clip(qrow - NA_WIN_ROWS // 2, 0, GRID_H - NA_WIN_ROWS)
        krow = span0 + (lax.broadcasted_iota(jnp.int32, (1, NA_SPAN), 1) >> 6)
        in_rows = jnp.logical_and(krow >= wrow, krow < wrow + NA_WIN_ROWS)
        for c in range(2):
            def a_score(q_h, w, c=c, row0=row0, span0=span0, nloc=nloc, in_rows=in_rows):
                cs = slice(c * LANES, (c + 1) * LANES)
                bias = jnp.concatenate(
                    [jnp.concatenate(
                        [bias_ref[2 * c + w,
                                  jnp.clip(span0 + 2 * p - (row0 + iq) + NA_WIN_ROWS, 0, N_PAIR - 1)]
                         for p in range(NA_SPAN_ROWS // 2)], axis=1)
                     for iq in range(NA_BLK_ROWS)], axis=0)
                s_loc = jnp.where(in_rows, _qk(q_h, ka_ref[nloc, cs]) + bias, NEG)
                return ([(s_loc, va_refs[w][nloc, cs]),
                         (_qk(q_h, ka_ref[ctx, cs]), va_refs[w][ctx, cs])], None)
            tasks += _head_tasks(rows, c * LANES, qa_ref, c, lane_lo, a_score)

    _run_heads(tasks, att_ref)
    o = jnp.dot(att_ref[...], wo_ref[...], preferred_element_type=F32)
    o_ref[...] = x_ref[...] + mod_ref[2:3, :] * o


def _ctx_attn_kernel(sink_ref, x_ref, mod_ref, wo_ref, qb_ref, qa_ref, qw_ref,
                     kb_ref, vba_ref, vbb_ref, kw_ref, vwa_ref, vwb_ref, ka_ref, vaa_ref, vab_ref,
                     o_ref, att_ref):
    lane = lax.broadcasted_iota(jnp.int32, (TM, LANES), 1)
    rope_a = ((lane >> 5) & 1) == 0
    lane_lo = lane < HEAD_DIM

    def attend(q_ref, chunks, a_mask, k, v_a, v_b, col, extra=None):
        halves = [_split_heads(q_ref[:, c * LANES:(c + 1) * LANES], a_mask) for c in chunks]
        n = len(chunks) * TM
        s = _qk(jnp.concatenate([h[0] for h in halves] + [h[1] for h in halves], axis=0), k)
        m = jnp.max(s, axis=-1, keepdims=True)
        if extra is not None:
            m = jnp.maximum(m, extra)
        p = jnp.exp2(s - m).astype(BF16)
        o = jnp.concatenate([jnp.dot(p[:n], v_a, preferred_element_type=F32),
                             jnp.dot(p[n:], v_b, preferred_element_type=F32)], axis=0)
        den = pltpu.roll(o, HEAD_DIM, 1)
        if extra is not None:
            den = den + jnp.exp2(extra - m)
        o = o / den
        for i in range(len(chunks)):
            att_ref[:, col + i * LANES:col + (i + 1) * LANES] = jnp.where(
                lane_lo, o[i * TM:(i + 1) * TM], o[n + i * TM:n + (i + 1) * TM]).astype(BF16)

    attend(qb_ref, range(4), rope_a, kb_ref[...], vba_ref[...], vbb_ref[...], 256)
    sinks = jnp.concatenate([jnp.full((TM, 1), sink_ref[h] * LOG2E, F32) for h in range(WA_HEADS)], axis=0)
    attend(qw_ref, range(2), rope_a, kw_ref[...], vwa_ref[...], vwb_ref[...], 768, extra=sinks)
    for c in range(2):
        cs = slice(c * LANES, (c + 1) * LANES)
        attend(qa_ref, [c], lane_lo, ka_ref[:, cs], vaa_ref[:, cs], vab_ref[:, cs], c * LANES)
    o = jnp.dot(att_ref[...], wo_ref[...], preferred_element_type=F32)
    o_ref[...] = x_ref[...] + mod_ref[2:3, :] * o


def _ffn_kernel(x_ref, xc_ref, xp_ref, xn_ref, g_ref, mod_ref, wup_ref, cw_ref, cb_ref, wdn_ref,
                fg_ref, o_ref, *, final):
    j = pl.program_id(1)
    xm = x_ref[...] if final else jnp.where(j < N_LAT, x_ref[...], xc_ref[...])
    xf = jnp.concatenate([xp_ref[...], xm, xn_ref[...]], axis=0)
    h = _norm_mod(xf, g_ref[...], mod_ref[3:4, :], mod_ref[4:5, :])
    af = jnp.dot(h.astype(BF16), wup_ref[:, :D_FF], preferred_element_type=F32)
    b = jnp.dot(h[HALO:HALO + TM].astype(BF16), wup_ref[:, D_FF:], preferred_element_type=F32)
    a_mid = af[HALO:HALO + TM]
    has_prev = jnp.logical_and(j != 0, j != N_LAT)
    has_next = jnp.logical_and(j != N_LAT - 1, j != N_BLK - 1)
    a_prev = jnp.where(has_prev, af[HALO - 1:HALO], 0.0)
    a_next = jnp.where(has_next, af[HALO + TM:HALO + TM + 1], 0.0)
    row = lax.broadcasted_iota(jnp.int32, (TM, 1), 0)
    a_up = jnp.where(row == 0, a_prev, pltpu.roll(a_mid, 1, 0))
    a_dn = jnp.where(row == TM - 1, a_next, pltpu.roll(a_mid, TM - 1, 0))
    a = a_up * cw_ref[0:1, :] + a_mid * cw_ref[1:2, :] + a_dn * cw_ref[2:3, :] + cb_ref[...]
    g = (a * jax.nn.sigmoid(a)) * b
    y = jnp.dot(g.astype(BF16), wdn_ref[...], preferred_element_type=F32)
    out = xm + mod_ref[5:6, :] * y
    if final:
        out = (out * lax.rsqrt(jnp.mean(out * out, axis=-1, keepdims=True) + EPS)) * fg_ref[...]
    o_ref[...] = out


def _row_spec(width, rows=TM):
    return pl.BlockSpec((None, rows, width), lambda b, j: (b, j, 0))


def _full_spec(width):
    return pl.BlockSpec((None, ROWS, width), lambda b, j: (b, 0, 0))


def _const_spec(shape):
    nd = len(shape)
    return pl.BlockSpec(shape, lambda b, j: (0,) * nd)


def _layer_spec(l, shape):
    nd = len(shape)
    return pl.BlockSpec((None,) + tuple(shape), lambda b, j: (l,) + (0,) * nd)


def _mod_spec(l):
    return pl.BlockSpec((None, None, N_MOD, D_MODEL),
                        lambda b, j: (l, jnp.where(j < N_LAT, b, BATCH), 0, 0))


def _act_shape(width, dtype=BF16):
    return jax.ShapeDtypeStruct((BATCH, ROWS, width), dtype)


def kernel(x, c, ctx, c_ctx, attn_norm, ffn_norm, w_mod, b_mod, w_in, q_gain, k_gain,
           na_rpb, wa_sink, w_out, w_up, conv_w, conv_b, w_down, final_norm):
    x_lat, x_ctx, ctx_blk = x, ctx, 0
    c_all = jnp.zeros((MOD_ROWS, D_MODEL), F32).at[:BATCH].set(c).at[BATCH].set(c_ctx)
    w_out_p = _take_runs(w_out.astype(BF16), _OUT_PERM, 1)
    w_up_b = w_up.astype(BF16)
    w_dn_b = w_down.astype(BF16)
    qg = q_gain[:, _GAIN_PERM].reshape(DEPTH, 1, LANES)
    kg = k_gain[:, _GAIN_PERM].reshape(DEPTH, 1, LANES)
    cos128, sin128 = _rope_tables()
    an = attn_norm.reshape(DEPTH, 1, D_MODEL)
    fn = ffn_norm.reshape(DEPTH, 1, D_MODEL)
    fg = final_norm.reshape(1, D_MODEL)
    cb = conv_b.reshape(DEPTH, 1, D_FF)
    bm = b_mod.reshape(DEPTH, 1, N_MOD * D_MODEL)

    bias_tab = pl.pallas_call(
        _na_bias_kernel,
        grid=(DEPTH,),
        in_specs=[pl.BlockSpec((None, NA_HEADS, N_PAIR, LANES), lambda l: (l, 0, 0, 0))],
        out_specs=pl.BlockSpec((None, NA_HEADS, N_PAIR, GRID_W, LANES), lambda l: (l, 0, 0, 0, 0)),
        out_shape=jax.ShapeDtypeStruct((DEPTH, NA_HEADS, N_PAIR, GRID_W, LANES), F32),
        compiler_params=_params(1),
        name="nbr_bias",
    )(_na_bias_rows(na_rpb))

    w_in_p = pl.pallas_call(
        _in_perm_kernel,
        grid=(DEPTH, D_MODEL // TM),
        in_specs=[pl.BlockSpec((None, TM, D_PROJ), lambda l, k: (l, k, 0))],
        out_specs=pl.BlockSpec((None, TM, D_PROJ), lambda l, k: (l, k, 0)),
        out_shape=jax.ShapeDtypeStruct((DEPTH, D_MODEL, D_PROJ), BF16),
        compiler_params=_params(2),
        name="w_in_perm",
    )(w_in)

    mods = pl.pallas_call(
        _mod_kernel,
        grid=(DEPTH, N_MOD),
        in_specs=[pl.BlockSpec((MOD_ROWS, D_MODEL), lambda l, n: (0, 0)),
                  pl.BlockSpec((None, D_MODEL, D_MODEL), lambda l, n: (l, 0, n)),
                  pl.BlockSpec((None, 1, D_MODEL), lambda l, n: (l, 0, n))],
        out_specs=pl.BlockSpec((None, MOD_ROWS, D_MODEL), lambda l, n: (l, 0, n)),
        out_shape=jax.ShapeDtypeStruct((DEPTH, MOD_ROWS, N_MOD * D_MODEL), F32),
        compiler_params=_params(2),
        name="adaln_mod",
    )(c_all, w_mod, bm)
    mods = mods.reshape(DEPTH, MOD_ROWS, N_MOD, D_MODEL)

    qkv_widths = (512, 256, 256, 128, 128, 128, 128, 128, 128, 256, 256, 256)
    for l in range(DEPTH):
        last = l == DEPTH - 1
        nq = N_LAT if last else N_BLK
        out_rows = SEQ if last else ROWS

        qkv = pl.pallas_call(
            _qkv_kernel,
            grid=(BATCH, 1),
            in_specs=[_row_spec(D_MODEL, SEQ),
                      pl.BlockSpec((None, CTX_LEN, D_MODEL), lambda b, j, k=ctx_blk: (b, k, 0)),
                      _layer_spec(l, (1, D_MODEL)),
                      pl.BlockSpec((None, None, N_MOD, D_MODEL), lambda b, j: (l, b, 0, 0)),
                      pl.BlockSpec((None, None, N_MOD, D_MODEL), lambda b, j: (l, BATCH, 0, 0)),
                      _layer_spec(l, (D_MODEL, D_PROJ)),
                      _const_spec((ROWS, LANES)), _const_spec((ROWS, LANES)),
                      _layer_spec(l, (1, LANES)), _layer_spec(l, (1, LANES))],
            out_specs=[_row_spec(w, ROWS) for w in qkv_widths],
            out_shape=[_act_shape(w) for w in qkv_widths],
            compiler_params=_params(2, VMEM_LIMIT),
            name=f"qkv_proj_{l}",
        )(x_lat, x_ctx, an, mods, mods, w_in_p, cos128, sin128, qg, kg)

        x1 = pl.pallas_call(
            _attn_kernel,
            grid=(BATCH, SEQ // ATT_TM),
            in_specs=[pl.BlockSpec(memory_space=pltpu.SMEM),
                      _row_spec(D_MODEL, ATT_TM), _mod_spec(l), _layer_spec(l, (D_MODEL, D_MODEL)),
                      _layer_spec(l, (NA_HEADS, N_PAIR, GRID_W, LANES)),
                      _row_spec(512, ATT_TM), _row_spec(256, ATT_TM), _row_spec(256, ATT_TM)]
                     + [_full_spec(w) for w in qkv_widths[3:]],
            out_specs=_row_spec(D_MODEL, ATT_TM),
            out_shape=jax.ShapeDtypeStruct((BATCH, SEQ, D_MODEL), F32),
            scratch_shapes=[pltpu.VMEM((ATT_TM, D_MODEL), BF16)],
            compiler_params=_params(2, VMEM_LIMIT),
            name=f"attn_{l}",
        )(wa_sink[l], x_lat, mods, w_out_p, bias_tab, *qkv)

        if last:
            x1c = x1
        else:
            ctx_spec = lambda w: pl.BlockSpec((None, TM, w), lambda b: (b, N_LAT, 0))
            x1c = pl.pallas_call(
                _ctx_attn_kernel,
                grid=(BATCH,),
                in_specs=[pl.BlockSpec(memory_space=pltpu.SMEM),
                          pl.BlockSpec((None, CTX_LEN, D_MODEL), lambda b, k=ctx_blk: (b, k, 0)),
                          pl.BlockSpec((None, None, N_MOD, D_MODEL), lambda b: (l, BATCH, 0, 0)),
                          pl.BlockSpec((None, D_MODEL, D_MODEL), lambda b: (l, 0, 0))]
                         + [ctx_spec(w) for w in qkv_widths],
                out_specs=pl.BlockSpec((None, CTX_LEN, D_MODEL), lambda b: (b, 0, 0)),
                out_shape=jax.ShapeDtypeStruct((BATCH, CTX_LEN, D_MODEL), F32),
                scratch_shapes=[pltpu.VMEM((TM, D_MODEL), BF16)],
                compiler_params=_params(1, VMEM_LIMIT),
                name=f"ctx_attn_{l}",
            )(wa_sink[l], x_ctx, mods, w_out_p, *qkv)

        blk8 = TM // HALO
        n_halo = SEQ // HALO
        x_lat = x_ctx = pl.pallas_call(
            functools.partial(_ffn_kernel, final=last),
            grid=(BATCH, nq),
            in_specs=[pl.BlockSpec((None, TM, D_MODEL),
                                   lambda b, j: (b, jnp.minimum(j, N_LAT - 1), 0)),
                      pl.BlockSpec((None, CTX_LEN, D_MODEL), lambda b, j: (b, 0, 0)),
                      pl.BlockSpec((None, HALO, D_MODEL),
                                   lambda b, j: (b, jnp.maximum(j * blk8 - 1, 0), 0)),
                      pl.BlockSpec((None, HALO, D_MODEL),
                                   lambda b, j: (b, jnp.minimum((j + 1) * blk8, n_halo - 1), 0)),
                      _layer_spec(l, (1, D_MODEL)), _mod_spec(l),
                      _layer_spec(l, (D_MODEL, 2 * D_FF)),
                      _layer_spec(l, (CONV_W, D_FF)), _layer_spec(l, (1, D_FF)),
                      _layer_spec(l, (D_FF, D_MODEL)), _const_spec((1, D_MODEL))],
            out_specs=_row_spec(D_MODEL),
            out_shape=jax.ShapeDtypeStruct((BATCH, out_rows, D_MODEL), F32),
            compiler_params=_params(2, VMEM_LIMIT),
            name=f"conv_ffn_{l}",
        )(x1, x1c, x1, x1, fn, mods, w_up_b, conv_w, cb, w_dn_b, fg)
        ctx_blk = N_LAT

    return x_lat
```

```python
import functools
import math

import numpy as np
import jax
import jax.numpy as jnp
from jax import lax
from jax.experimental import pallas as pl
from jax.experimental.pallas import tpu as pltpu

D_MODEL = 1024
BATCH = 8
SEQ = 2048
DEPTH = 4
GRID_W = 64
CTX_LEN = 256
HEAD_DIM = 64
NA_HEADS = 4
GA_HEADS = 8
GA_KV_HEADS = 2
WA_HEADS = 4
WA_KV_HEADS = 2
NA_WIN_ROWS = 8
NA_WIN_COLS = 16
WA_RADIUS = 128
D_FF = 2816
CONV_W = 3
ROPE_THETA = 10000.0
EPS = 1e-6
N_MOD = 6
D_PROJ = 2048

ROWS = SEQ + CTX_LEN
TM = 256
N_BLK = ROWS // TM
N_LAT = SEQ // TM
GRID_H = SEQ // GRID_W
LANES = 128
HALO = 8
MOD_ROWS = 16
NEG = -1e30
LOG2E = math.log2(math.e)
VMEM_LIMIT = 56 * 1024 * 1024

WA_SPAN = TM + 2 * WA_RADIUS
NA_BLK_ROWS = TM // GRID_W
NA_SPAN_ROWS = 12
NA_SPAN = NA_SPAN_ROWS * GRID_W
N_DR = 2 * NA_WIN_ROWS - 1
N_DC = 2 * NA_WIN_COLS - 1
N_PAIR = 16
ATT_TM = 512
CTX_NB = 4
ATTN_AHEAD = 1
QKV_SUB = 256

F32 = jnp.float32
BF16 = jnp.bfloat16


def _params(n_axes, vmem=None):
    return pltpu.CompilerParams(dimension_semantics=("arbitrary",) * n_axes,
                                vmem_limit_bytes=vmem)


def _rope_lane_perm(base, head_stride, n_chunks):
    cols = []
    for c in range(n_chunks):
        for lane in range(LANES):
            part, hsel, i = lane // 64, (lane // 32) % 2, lane % 32
            cols.append(base + HEAD_DIM * (c + head_stride * hsel) + 32 * part + i)
    return cols


def _in_proj_perm():
    qa, ka, va = 0, 256, 512
    qb, kb, vb = 768, 1280, 1408
    qw, kw, vw = 1536, 1792, 1920
    cols = []
    cols += _rope_lane_perm(qb, 4, 4)
    cols += list(range(qa, qa + 256))
    cols += list(range(ka, ka + 256))
    cols += list(range(va, va + 256))
    cols += _rope_lane_perm(qw, 2, 2)
    cols += _rope_lane_perm(kb, 1, 1)
    cols += list(range(vb, vb + 128))
    cols += _rope_lane_perm(kw, 1, 1)
    cols += list(range(vw, vw + 128))
    return np.asarray(cols, np.int32)


def _out_proj_perm():
    rows = list(range(256))
    for c in range(4):
        for lane in range(LANES):
            rows.append(256 + HEAD_DIM * (c + 4 * (lane // 64)) + lane % 64)
    for c in range(2):
        for lane in range(LANES):
            rows.append(768 + HEAD_DIM * (c + 2 * (lane // 64)) + lane % 64)
    return np.asarray(rows, np.int32)


_IN_PERM = _in_proj_perm()
_OUT_PERM = _out_proj_perm()
_GAIN_PERM = np.asarray([32 * (l // 64) + l % 32 for l in range(LANES)], np.int32)
_PAIR_ROW = np.clip(np.arange(N_PAIR + 1) - 1, 0, N_DR - 1)


def _take_runs(a, perm, axis):
    cuts = [0] + [i for i in range(1, len(perm)) if perm[i] != perm[i - 1] + 1] + [len(perm)]
    return jnp.concatenate(
        [lax.slice_in_dim(a, int(perm[s]), int(perm[e - 1]) + 1, axis=axis)
         for s, e in zip(cuts[:-1], cuts[1:])], axis=axis)


def _rope_tables():
    t = jnp.arange(SEQ, dtype=jnp.int32)
    row = (t // GRID_W).astype(F32)
    col = (t % GRID_W).astype(F32)
    n = HEAD_DIM // 4
    inv = ROPE_THETA ** (-jnp.arange(n, dtype=F32) / n)
    ang = jnp.concatenate([row[:, None] * inv, col[:, None] * inv], axis=-1)
    cos, sin = jnp.cos(ang), jnp.sin(ang)
    cos128 = jnp.concatenate([cos, cos, cos, cos], axis=-1)
    sin128 = jnp.concatenate([-sin, -sin, sin, sin], axis=-1)
    cos128 = jnp.concatenate([cos128, jnp.ones((CTX_LEN, LANES), F32)], axis=0)
    sin128 = jnp.concatenate([sin128, jnp.zeros((CTX_LEN, LANES), F32)], axis=0)
    return cos128, sin128


def _na_bias_rows(na_rpb):
    rp = jnp.pad(na_rpb, ((0, 0), (0, 0), (0, 0), (0, GRID_W - N_DC)))
    return jnp.concatenate([rp[:, :, _PAIR_ROW[:-1]], rp[:, :, _PAIR_ROW[1:]]], axis=-1)


def _na_bias_kernel(t_ref, o_ref):
    kc = lax.broadcasted_iota(jnp.int32, (N_PAIR, LANES), 1) & (GRID_W - 1)
    for h in range(NA_HEADS):
        t = t_ref[h] * LOG2E
        for c in range(GRID_W):
            ws = min(max(c - NA_WIN_COLS // 2, 0), GRID_W - NA_WIN_COLS)
            rolled = pltpu.roll(t, (LANES - (NA_WIN_COLS - 1) + c) % LANES, 1)
            o_ref[h, :, c, :] = jnp.where(jnp.logical_and(kc >= ws, kc < ws + NA_WIN_COLS), rolled, NEG)


def _in_perm_kernel(w_ref, o_ref):
    lane = lax.broadcasted_iota(jnp.int32, (w_ref.shape[0], LANES), 1)
    piece = 32
    for oc in range(D_PROJ // LANES):
        starts = [int(_IN_PERM[oc * LANES + piece * p]) for p in range(LANES // piece)]
        if all(s == starts[0] + piece * p for p, s in enumerate(starts)) and starts[0] % LANES == 0:
            acc = w_ref[:, starts[0]:starts[0] + LANES]
        else:
            acc = None
            for p, s in enumerate(starts):
                chunk, off = divmod(s, LANES)
                v = w_ref[:, chunk * LANES:(chunk + 1) * LANES]
                shift = (piece * p - off) % LANES
                v = pltpu.roll(v, shift, 1) if shift else v
                acc = v if acc is None else jnp.where(lane >= piece * p, v, acc)
        o_ref[:, oc * LANES:(oc + 1) * LANES] = acc.astype(BF16)


def _mod_kernel(c_ref, w_ref, b_ref, o_ref):
    c = c_ref[...]
    act = (c * jax.nn.sigmoid(c)).astype(BF16)
    o_ref[...] = jnp.dot(act, w_ref[...].astype(BF16), preferred_element_type=F32) + b_ref[...]


def _norm_mod(x, gain, shift, scale):
    y = x * lax.rsqrt(jnp.mean(x * x, axis=-1, keepdims=True) + EPS)
    return (y * gain) * (1 + scale) + shift


def _qkv_kernel(x_ref, xc_ref, g_ref, mod_ref, cmod_ref, w_ref, cos_ref, sin_ref, qg_ref, kg_ref,
                qb_ref, qa_ref, qw_ref, kb_ref, vba_ref, vbb_ref, kw_ref, vwa_ref, vwb_ref,
                ka_ref, vaa_ref, vab_ref):
    sub = QKV_SUB
    lane = lax.broadcasted_iota(jnp.int32, (sub, LANES), 1)
    head_a = ((lane >> 5) & 1) == 0
    lane_lo = lane < HEAD_DIM
    qscale = HEAD_DIM ** -0.5 * LOG2E

    def headnorm(xc, gain):
        sq = xc * xc
        sa = jnp.sum(jnp.where(head_a, sq, 0.0), axis=-1, keepdims=True)
        sb = jnp.sum(jnp.where(head_a, 0.0, sq), axis=-1, keepdims=True)
        ms = jnp.where(head_a, sa, sb) * (1.0 / HEAD_DIM)
        return (xc * lax.rsqrt(ms + EPS)) * gain

    def epilogue(y, rows):
        cosv, sinv = cos_ref[rows, :], sin_ref[rows, :]

        def rope(xc):
            return xc * cosv + pltpu.roll(xc, 64, 1) * sinv

        def chunk(c0):
            return y[:, c0:c0 + LANES]

        def put_v(va_ref, vb_ref, c, vc):
            va_ref[rows, c * LANES:(c + 1) * LANES] = jnp.where(lane_lo, vc, 1.0).astype(BF16)
            vb_ref[rows, c * LANES:(c + 1) * LANES] = jnp.where(lane_lo, 1.0, vc).astype(BF16)

        for c in range(4):
            qb_ref[rows, c * LANES:(c + 1) * LANES] = (
                rope(headnorm(chunk(c * LANES), qg_ref[...])) * qscale).astype(BF16)
        qa_ref[rows, :] = (y[:, 512:768] * qscale).astype(BF16)
        ka_ref[rows, :] = y[:, 768:1024].astype(BF16)
        for c in range(2):
            put_v(vaa_ref, vab_ref, c, chunk(1024 + c * LANES))
            qw_ref[rows, c * LANES:(c + 1) * LANES] = (
                rope(chunk(1280 + c * LANES)) * qscale).astype(BF16)
        kb_ref[rows, :] = rope(headnorm(chunk(1536), kg_ref[...])).astype(BF16)
        put_v(vba_ref, vbb_ref, 0, chunk(1664))
        kw_ref[rows, :] = rope(chunk(1792)).astype(BF16)
        put_v(vwa_ref, vwb_ref, 0, chunk(1920))

    pending = None
    for s in range(ROWS // sub):
        rows = slice(s * sub, (s + 1) * sub)
        if s * sub < SEQ:
            xs, m_ref = x_ref[rows, :], mod_ref
        else:
            xs, m_ref = xc_ref[s * sub - SEQ:(s + 1) * sub - SEQ, :], cmod_ref
        h = _norm_mod(xs, g_ref[...], m_ref[0:1, :], m_ref[1:2, :])
        y = jnp.dot(h.astype(BF16), w_ref[...], preferred_element_type=F32)
        if pending is not None:
            epilogue(*pending)
        pending = (y, rows)
    epilogue(*pending)


def _qk(q, k):
    return lax.dot_general(q, k, (((1,), (1,)), ((), ())), preferred_element_type=F32)


def _softmax_pv(parts, extra=None):
    m = None
    for s, _ in parts:
        sm = jnp.max(s, axis=-1, keepdims=True)
        m = sm if m is None else jnp.maximum(m, sm)
    if extra is not None:
        m = jnp.maximum(m, extra)
    o = None
    for s, v in parts:
        pv = jnp.dot(jnp.exp2(s - m).astype(BF16), v, preferred_element_type=F32)
        o = pv if o is None else o + pv
    den = pltpu.roll(o, HEAD_DIM, 1)
    if extra is not None:
        den = den + jnp.exp2(extra - m)
    return o / den


def _split_heads(q, a_mask):
    qf = q.astype(F32)
    return jnp.where(a_mask, qf, 0.0).astype(BF16), jnp.where(a_mask, 0.0, qf).astype(BF16)


def _head_tasks(rows, col, q_ref, c, a_mask, score):
    def task(w):
        return score(_split_heads(q_ref[rows, c * LANES:(c + 1) * LANES], a_mask)[w], w)
    return [(rows, col, w, functools.partial(task, w)) for w in range(2)]


def _run_heads(tasks, att_ref):
    lane_lo = lax.broadcasted_iota(jnp.int32, (TM, LANES), 1) < HEAD_DIM
    held = {}
    queue = [task[3]() for task in tasks[:ATTN_AHEAD]]
    for t, (rows, col, which, _) in enumerate(tasks):
        if t + ATTN_AHEAD < len(tasks):
            queue.append(tasks[t + ATTN_AHEAD][3]())
        o = _softmax_pv(*queue.pop(0))
        if which == 0:
            held[(rows.start, col)] = o
        else:
            att_ref[rows, col:col + LANES] = jnp.where(
                lane_lo, held.pop((rows.start, col)), o).astype(BF16)


def _attn_kernel(sink_ref, x_ref, mod_ref, wo_ref, bias_ref, qb_ref, qa_ref, qw_ref,
                 kb_ref, vba_ref, vbb_ref, kw_ref, vwa_ref, vwb_ref, ka_ref, vaa_ref, vab_ref,
                 o_ref, att_ref):
    lane = lax.broadcasted_iota(jnp.int32, (TM, LANES), 1)
    rope_a = ((lane >> 5) & 1) == 0
    lane_lo = lane < HEAD_DIM
    ctx = slice(SEQ, ROWS)
    vb_refs, vw_refs, va_refs = (vba_ref, vbb_ref), (vwa_ref, vwb_ref), (vaa_ref, vab_ref)

    tasks = []
    for sb in range(ATT_TM // TM):
        j = pl.program_id(1) * (ATT_TM // TM) + sb
        rows = slice(sb * TM, (sb + 1) * TM)

        for c in range(4):
            tasks += _head_tasks(rows, 256 + c * LANES, qb_ref, c, rope_a,
                                 lambda q_h, w: ([(_qk(q_h, kb_ref[...]), vb_refs[w][...])], None))

        start = pl.multiple_of(jnp.clip(j * TM - WA_RADIUS, 0, SEQ - WA_SPAN), WA_RADIUS)
        loc = pl.ds(start, WA_SPAN)
        qpos = j * TM + lax.broadcasted_iota(jnp.int32, (TM, 1), 0)
        kpos = start + lax.broadcasted_iota(jnp.int32, (1, WA_SPAN), 1)
        near = jnp.abs(qpos - kpos) <= WA_RADIUS
        for c in range(2):
            def w_score(q_h, w, c=c, loc=loc, near=near):
                return ([(jnp.where(near, _qk(q_h, kw_ref[loc, :]), NEG), vw_refs[w][loc, :]),
                         (_qk(q_h, kw_ref[ctx, :]), vw_refs[w][ctx, :])],
                        sink_ref[c + 2 * w] * LOG2E)
            tasks += _head_tasks(rows, 768 + c * LANES, qw_ref, c, rope_a, w_score)

        row0 = j * NA_BLK_ROWS
        span0 = jnp.clip(row0 - NA_WIN_ROWS // 2, 0, GRID_H - NA_SPAN_ROWS)
        nloc = pl.ds(pl.multiple_of(span0 * GRID_W, GRID_W), NA_SPAN)
        qrow = row0 + (lax.broadcasted_iota(jnp.int32, (TM, 1), 0) >> 6)
        wrow = jnp.clip(qrow - NA_WIN_ROWS // 2, 0, GRID_H - NA_WIN_ROWS)
        krow = span0 + (lax.broadcasted_iota(jnp.int32, (1, NA_SPAN), 1) >> 6)
        in_rows = jnp.logical_and(krow >= wrow, krow < wrow + NA_WIN_ROWS)
        for c in range(2):
            def a_score(q_h, w, c=c, row0=row0, span0=span0, nloc=nloc, in_rows=in_rows):
                cs = slice(c * LANES, (c + 1) * LANES)
                bias = jnp.concatenate(
                    [jnp.concatenate(
                        [bias_ref[2 * c + w,
                                  jnp.clip(span0 + 2 * p - (row0 + iq) + NA_WIN_ROWS, 0, N_PAIR - 1)]
                         for p in range(NA_SPAN_ROWS // 2)], axis=1)
                     for iq in range(NA_BLK_ROWS)], axis=0)
                s_loc = jnp.where(in_rows, _qk(q_h, ka_ref[nloc, cs]) + bias, NEG)
                return ([(s_loc, va_refs[w][nloc, cs]),
                         (_qk(q_h, ka_ref[ctx, cs]), va_refs[w][ctx, cs])], None)
            tasks += _head_tasks(rows, c * LANES, qa_ref, c, lane_lo, a_score)

    _run_heads(tasks, att_ref)
    o = jnp.dot(att_ref[...], wo_ref[...], preferred_element_type=F32)
    o_ref[...] = x_ref[...] + mod_ref[2:3, :] * o


def _ctx_attn_kernel(sink_ref, x_ref, mod_ref, wo_ref, qb_ref, qa_ref, qw_ref,
                     kb_ref, vba_ref, vbb_ref, kw_ref, vwa_ref, vwb_ref, ka_ref, vaa_ref, vab_ref,
                     o_ref, att_ref):
    lane = lax.broadcasted_iota(jnp.int32, (TM, LANES), 1)
    rope_a = ((lane >> 5) & 1) == 0
    lane_lo = lane < HEAD_DIM

    def attend(nb, q_ref, chunks, a_mask, k, v_a, v_b, col, extra=None):
        rows = slice(nb * TM, (nb + 1) * TM)
        halves = [_split_heads(q_ref[nb, :, c * LANES:(c + 1) * LANES], a_mask) for c in chunks]
        n = len(chunks) * TM
        s = _qk(jnp.concatenate([h[0] for h in halves] + [h[1] for h in halves], axis=0), k)
        m = jnp.max(s, axis=-1, keepdims=True)
        if extra is not None:
            m = jnp.maximum(m, extra)
        p = jnp.exp2(s - m).astype(BF16)
        o = jnp.concatenate([jnp.dot(p[:n], v_a, preferred_element_type=F32),
                             jnp.dot(p[n:], v_b, preferred_element_type=F32)], axis=0)
        den = pltpu.roll(o, HEAD_DIM, 1)
        if extra is not None:
            den = den + jnp.exp2(extra - m)
        o = o / den
        for i in range(len(chunks)):
            att_ref[rows, col + i * LANES:col + (i + 1) * LANES] = jnp.where(
                lane_lo, o[i * TM:(i + 1) * TM], o[n + i * TM:n + (i + 1) * TM]).astype(BF16)

    sinks = jnp.concatenate([jnp.full((TM, 1), sink_ref[h] * LOG2E, F32) for h in range(WA_HEADS)], axis=0)
    for nb in range(CTX_NB):
        attend(nb, qb_ref, range(4), rope_a, kb_ref[nb], vba_ref[nb], vbb_ref[nb], 256)
        attend(nb, qw_ref, range(2), rope_a, kw_ref[nb], vwa_ref[nb], vwb_ref[nb], 768, extra=sinks)
        for c in range(2):
            cs = slice(c * LANES, (c + 1) * LANES)
            attend(nb, qa_ref, [c], lane_lo, ka_ref[nb, :, cs], vaa_ref[nb, :, cs], vab_ref[nb, :, cs],
                   c * LANES)
    o = jnp.dot(att_ref[...], wo_ref[...], preferred_element_type=F32)
    for nb in range(CTX_NB):
        o_ref[nb] = x_ref[nb] + mod_ref[2:3, :] * o[nb * TM:(nb + 1) * TM]


def _ffn_kernel(x_ref, xc_ref, xp_ref, xn_ref, g_ref, mod_ref, wup_ref, cw_ref, cb_ref, wdn_ref,
                fg_ref, o_ref, *, final):
    j = pl.program_id(1)
    xm = x_ref[...] if final else jnp.where(j < N_LAT, x_ref[...], xc_ref[...])
    xf = jnp.concatenate([xp_ref[...], xm, xn_ref[...]], axis=0)
    h = _norm_mod(xf, g_ref[...], mod_ref[3:4, :], mod_ref[4:5, :])
    af = jnp.dot(h.astype(BF16), wup_ref[:, :D_FF], preferred_element_type=F32)
    b = jnp.dot(h[HALO:HALO + TM].astype(BF16), wup_ref[:, D_FF:], preferred_element_type=F32)
    a_mid = af[HALO:HALO + TM]
    has_prev = jnp.logical_and(j != 0, j != N_LAT)
    has_next = jnp.logical_and(j != N_LAT - 1, j != N_BLK - 1)
    a_prev = jnp.where(has_prev, af[HALO - 1:HALO], 0.0)
    a_next = jnp.where(has_next, af[HALO + TM:HALO + TM + 1], 0.0)
    row = lax.broadcasted_iota(jnp.int32, (TM, 1), 0)
    a_up = jnp.where(row == 0, a_prev, pltpu.roll(a_mid, 1, 0))
    a_dn = jnp.where(row == TM - 1, a_next, pltpu.roll(a_mid, TM - 1, 0))
    a = a_up * cw_ref[0:1, :] + a_mid * cw_ref[1:2, :] + a_dn * cw_ref[2:3, :] + cb_ref[...]
    g = (a * jax.nn.sigmoid(a)) * b
    y = jnp.dot(g.astype(BF16), wdn_ref[...], preferred_element_type=F32)
    out = xm + mod_ref[5:6, :] * y
    if final:
        out = (out * lax.rsqrt(jnp.mean(out * out, axis=-1, keepdims=True) + EPS)) * fg_ref[...]
    o_ref[...] = out


def _row_spec(width, rows=TM):
    return pl.BlockSpec((None, rows, width), lambda b, j: (b, j, 0))


def _full_spec(width):
    return pl.BlockSpec((None, ROWS, width), lambda b, j: (b, 0, 0))


def _const_spec(shape):
    nd = len(shape)
    return pl.BlockSpec(shape, lambda b, j: (0,) * nd)


def _layer_spec(l, shape):
    nd = len(shape)
    return pl.BlockSpec((None,) + tuple(shape), lambda b, j: (l,) + (0,) * nd)


def _mod_spec(l):
    return pl.BlockSpec((None, None, N_MOD, D_MODEL),
                        lambda b, j: (l, jnp.where(j < N_LAT, b, BATCH), 0, 0))


def _act_shape(width, dtype=BF16):
    return jax.ShapeDtypeStruct((BATCH, ROWS, width), dtype)


def kernel(x, c, ctx, c_ctx, attn_norm, ffn_norm, w_mod, b_mod, w_in, q_gain, k_gain,
           na_rpb, wa_sink, w_out, w_up, conv_w, conv_b, w_down, final_norm):
    x_lat, x_ctx, ctx_blk = x, ctx, 0
    c_all = jnp.zeros((MOD_ROWS, D_MODEL), F32).at[:BATCH].set(c).at[BATCH].set(c_ctx)
    w_out_p = _take_runs(w_out.astype(BF16), _OUT_PERM, 1)
    w_up_b = w_up.astype(BF16)
    w_dn_b = w_down.astype(BF16)
    qg = q_gain[:, _GAIN_PERM].reshape(DEPTH, 1, LANES)
    kg = k_gain[:, _GAIN_PERM].reshape(DEPTH, 1, LANES)
    cos128, sin128 = _rope_tables()
    an = attn_norm.reshape(DEPTH, 1, D_MODEL)
    fn = ffn_norm.reshape(DEPTH, 1, D_MODEL)
    fg = final_norm.reshape(1, D_MODEL)
    cb = conv_b.reshape(DEPTH, 1, D_FF)
    bm = b_mod.reshape(DEPTH, 1, N_MOD * D_MODEL)

    bias_tab = pl.pallas_call(
        _na_bias_kernel,
        grid=(DEPTH,),
        in_specs=[pl.BlockSpec((None, NA_HEADS, N_PAIR, LANES), lambda l: (l, 0, 0, 0))],
        out_specs=pl.BlockSpec((None, NA_HEADS, N_PAIR, GRID_W, LANES), lambda l: (l, 0, 0, 0, 0)),
        out_shape=jax.ShapeDtypeStruct((DEPTH, NA_HEADS, N_PAIR, GRID_W, LANES), F32),
        compiler_params=_params(1),
        name="nbr_bias",
    )(_na_bias_rows(na_rpb))

    w_in_p = pl.pallas_call(
        _in_perm_kernel,
        grid=(DEPTH, D_MODEL // TM),
        in_specs=[pl.BlockSpec((None, TM, D_PROJ), lambda l, k: (l, k, 0))],
        out_specs=pl.BlockSpec((None, TM, D_PROJ), lambda l, k: (l, k, 0)),
        out_shape=jax.ShapeDtypeStruct((DEPTH, D_MODEL, D_PROJ), BF16),
        compiler_params=_params(2),
        name="w_in_perm",
    )(w_in)

    mods = pl.pallas_call(
        _mod_kernel,
        grid=(DEPTH, N_MOD),
        in_specs=[pl.BlockSpec((MOD_ROWS, D_MODEL), lambda l, n: (0, 0)),
                  pl.BlockSpec((None, D_MODEL, D_MODEL), lambda l, n: (l, 0, n)),
                  pl.BlockSpec((None, 1, D_MODEL), lambda l, n: (l, 0, n))],
        out_specs=pl.BlockSpec((None, MOD_ROWS, D_MODEL), lambda l, n: (l, 0, n)),
        out_shape=jax.ShapeDtypeStruct((DEPTH, MOD_ROWS, N_MOD * D_MODEL), F32),
        compiler_params=_params(2),
        name="adaln_mod",
    )(c_all, w_mod, bm)
    mods = mods.reshape(DEPTH, MOD_ROWS, N_MOD, D_MODEL)

    qkv_widths = (512, 256, 256, 128, 128, 128, 128, 128, 128, 256, 256, 256)
    for l in range(DEPTH):
        last = l == DEPTH - 1
        nq = N_LAT if last else N_BLK
        out_rows = SEQ if last else ROWS

        qkv = pl.pallas_call(
            _qkv_kernel,
            grid=(BATCH, 1),
            in_specs=[_row_spec(D_MODEL, SEQ),
                      pl.BlockSpec((None, CTX_LEN, D_MODEL), lambda b, j, k=ctx_blk: (b, k, 0)),
                      _layer_spec(l, (1, D_MODEL)),
                      pl.BlockSpec((None, None, N_MOD, D_MODEL), lambda b, j: (l, b, 0, 0)),
                      pl.BlockSpec((None, None, N_MOD, D_MODEL), lambda b, j: (l, BATCH, 0, 0)),
                      _layer_spec(l, (D_MODEL, D_PROJ)),
                      _const_spec((ROWS, LANES)), _const_spec((ROWS, LANES)),
                      _layer_spec(l, (1, LANES)), _layer_spec(l, (1, LANES))],
            out_specs=[_row_spec(w, ROWS) for w in qkv_widths],
            out_shape=[_act_shape(w) for w in qkv_widths],
            compiler_params=_params(2, VMEM_LIMIT),
            name=f"qkv_proj_{l}",
        )(x_lat, x_ctx, an, mods, mods, w_in_p, cos128, sin128, qg, kg)

        x1 = pl.pallas_call(
            _attn_kernel,
            grid=(BATCH, SEQ // ATT_TM),
            in_specs=[pl.BlockSpec(memory_space=pltpu.SMEM),
                      _row_spec(D_MODEL, ATT_TM), _mod_spec(l), _layer_spec(l, (D_MODEL, D_MODEL)),
                      _layer_spec(l, (NA_HEADS, N_PAIR, GRID_W, LANES)),
                      _row_spec(512, ATT_TM), _row_spec(256, ATT_TM), _row_spec(256, ATT_TM)]
                     + [_full_spec(w) for w in qkv_widths[3:]],
            out_specs=_row_spec(D_MODEL, ATT_TM),
            out_shape=jax.ShapeDtypeStruct((BATCH, SEQ, D_MODEL), F32),
            scratch_shapes=[pltpu.VMEM((ATT_TM, D_MODEL), BF16)],
            compiler_params=_params(2, VMEM_LIMIT),
            name=f"attn_{l}",
        )(wa_sink[l], x_lat, mods, w_out_p, bias_tab, *qkv)

        if last:
            x1c = x1
        else:
            ctx_spec = lambda w: pl.BlockSpec((CTX_NB, TM, w), lambda b: (b, N_LAT, 0))
            x1c = pl.pallas_call(
                _ctx_attn_kernel,
                grid=(BATCH // CTX_NB,),
                in_specs=[pl.BlockSpec(memory_space=pltpu.SMEM),
                          pl.BlockSpec((CTX_NB, CTX_LEN, D_MODEL), lambda b, k=ctx_blk: (b, k, 0)),
                          pl.BlockSpec((None, None, N_MOD, D_MODEL), lambda b: (l, BATCH, 0, 0)),
                          pl.BlockSpec((None, D_MODEL, D_MODEL), lambda b: (l, 0, 0))]
                         + [ctx_spec(w) for w in qkv_widths],
                out_specs=pl.BlockSpec((CTX_NB, CTX_LEN, D_MODEL), lambda b: (b, 0, 0)),
                out_shape=jax.ShapeDtypeStruct((BATCH, CTX_LEN, D_MODEL), F32),
                scratch_shapes=[pltpu.VMEM((CTX_NB * TM, D_MODEL), BF16)],
                compiler_params=_params(1, VMEM_LIMIT),
                name=f"ctx_attn_{l}",
            )(wa_sink[l], x_ctx, mods, w_out_p, *qkv)

        blk8 = TM // HALO
        n_halo = SEQ // HALO
        x_lat = x_ctx = pl.pallas_call(
            functools.partial(_ffn_kernel, final=last),
            grid=(BATCH, nq),
            in_specs=[pl.BlockSpec((None, TM, D_MODEL),
                                   lambda b, j: (b, jnp.minimum(j, N_LAT - 1), 0)),
                      pl.BlockSpec((None, CTX_LEN, D_MODEL), lambda b, j: (b, 0, 0)),
                      pl.BlockSpec((None, HALO, D_MODEL),
                                   lambda b, j: (b, jnp.maximum(j * blk8 - 1, 0), 0)),
                      pl.BlockSpec((None, HALO, D_MODEL),
                                   lambda b, j: (b, jnp.minimum((j + 1) * blk8, n_halo - 1), 0)),
                      _layer_spec(l, (1, D_MODEL)), _mod_spec(l),
                      _layer_spec(l, (D_MODEL, 2 * D_FF)),
                      _layer_spec(l, (CONV_W, D_FF)), _layer_spec(l, (1, D_FF)),
                      _layer_spec(l, (D_FF, D_MODEL)), _const_spec((1, D_MODEL))],
            out_specs=_row_spec(D_MODEL),
            out_shape=jax.ShapeDtypeStruct((BATCH, out_rows, D_MODEL), F32),
            compiler_params=_params(2, VMEM_LIMIT),
            name=f"conv_ffn_{l}",
        )(x1, x1c, x1, x1, fn, mods, w_up_b, conv_w, cb, w_dn_b, fg)
        ctx_blk = N_LAT

    return x_lat
```
